```python
import math
import jax
import jax.numpy as jnp
from jax import lax
import numpy as np

D_MODEL = 1024
BATCH = 4
SEQ = 8192
DEPTH = 4

CHUNK = 64
N_MIXERS = 3
NL_A = len(range(0, DEPTH, N_MIXERS))
NL_B = len(range(1, DEPTH, N_MIXERS))
NL_C = len(range(2, DEPTH, N_MIXERS))
NORM_EPS = 1e-6

DK_A = 128
DV_A = 128
H_A = D_MODEL // DK_A
KEY_A = H_A * DK_A
VAL_A = H_A * DV_A
CONV_A = 4
QKV_A = 2 * KEY_A + VAL_A
IN_A = QKV_A + VAL_A + 2 * H_A
DK_B = 128
H_B = D_MODEL // DK_B
DV_B = D_MODEL // H_B
KEY_B = H_B * DK_B
VAL_B = H_B * DV_B
IN_B = 2 * KEY_B + 2 * VAL_B
HS_C = 64
H_C = D_MODEL // HS_C
LORA_W = 64
LORA_A = 64
LORA_G = 128
GN_EPS = 1e-5 * HS_C * HS_C
D_FF = ((8 * D_MODEL) // 3 + 255) // 256 * 256
CONV_F = 3

kernel_name = 'hybrid_deltanet_hgrn2_rwkv7_convffn_trunk'


def rms_norm(x, g):
    xf = x.astype(jnp.float32)
    y = xf * lax.rsqrt(jnp.mean(xf * xf, axis=-1, keepdims=True) + NORM_EPS)
    return (y * g.astype(jnp.float32)).astype(x.dtype)


def l2norm(x):
    return x * lax.rsqrt(jnp.sum(x * x, axis=-1, keepdims=True) + 1e-6)


def causal_dwconv(x, w):
    K = w.shape[0]
    T = x.shape[1]
    xp = jnp.pad(x, ((0, 0), (K - 1, 0), (0, 0)))
    y = xp[:, 0:T] * w[0]
    for kk in range(1, K):
        y = y + xp[:, kk:kk + T] * w[kk]
    return y


def to_chunks(t):
    b, s, h, d = t.shape
    return t.reshape(b, s // CHUNK, CHUNK, h, d).transpose(1, 0, 3, 2, 4)


def from_chunks(t):
    n, b, h, c, d = t.shape
    return t.transpose(1, 0, 3, 2, 4).reshape(b, n * c, h, d)


def chunk_gated_delta_rule(q, k, v, beta, g):
    dk = q.shape[-1]
    q, k, v = to_chunks(q), to_chunks(k), to_chunks(v)
    beta = to_chunks(beta[..., None])
    G = jnp.cumsum(to_chunks(g[..., None])[..., 0], axis=-1)
    idx = jnp.arange(CHUNK)
    incl = idx[:, None] >= idx[None, :]
    strict = idx[:, None] > idx[None, :]
    decay = jnp.exp(jnp.where(incl, G[..., :, None] - G[..., None, :], -jnp.inf))
    kb = k * beta
    L = jnp.where(strict, jnp.einsum('nbhid,nbhjd->nbhij', kb, k) * decay, 0.0)
    eye = jnp.eye(CHUNK, dtype=L.dtype)
    rhs = jnp.concatenate([kb * jnp.exp(G)[..., None], v * beta], axis=-1)
    wu = lax.linalg.triangular_solve(eye + L, rhs, left_side=True, lower=True, unit_diagonal=True)
    w, u = wu[..., :dk], wu[..., dk:]
    attn = jnp.einsum('nbhid,nbhjd->nbhij', q, k) * decay
    q_dec = q * jnp.exp(G)[..., None]
    k_dec = k * jnp.exp(G[..., -1:] - G)[..., None]
    g_last = jnp.exp(G[..., -1])[..., None, None]

    def step(S, xs):
        w_n, u_n, q_n, k_n, a_n, gl_n = xs
        v_new = u_n - w_n @ S
        o_n = q_n @ S + a_n @ v_new
        S = S * gl_n + jnp.swapaxes(k_n, -1, -2) @ v_new
        return S, o_n

    S0 = jnp.zeros(q.shape[1:3] + (dk, v.shape[-1]), jnp.float32)
    _, o = lax.scan(step, S0, (w, u, q_dec, k_dec, attn, g_last))
    return from_chunks(o)


def chunk_gla(q, k, v, log_f):
    q, k, v, log_f = to_chunks(q), to_chunks(k), to_chunks(v), to_chunks(log_f)
    bc = jnp.cumsum(log_f, axis=-2)
    q_dec = q * jnp.exp(bc)
    k_dec = k * jnp.exp(bc[..., -1:, :] - bc)
    f_last = jnp.exp(bc[..., -1, :])[..., :, None]
    idx = jnp.arange(CHUNK)
    incl = (idx[:, None] >= idx[None, :])[:, :, None]

    def step(S, xs):
        q_n, k_n, v_n, b_n, qd_n, kd_n, fl_n = xs
        dec = jnp.exp(jnp.where(incl, b_n[..., :, None, :] - b_n[..., None, :, :], -jnp.inf))
        a_n = jnp.einsum('bhid,bhjd,bhijd->bhij', q_n, k_n, dec)
        o_n = qd_n @ S + a_n @ v_n
        S = S * fl_n + jnp.swapaxes(kd_n, -1, -2) @ v_n
        return S, o_n

    S0 = jnp.zeros(q.shape[1:3] + (q.shape[-1], v.shape[-1]), jnp.float32)
    _, o = lax.scan(step, S0, (q, k, v, bc, q_dec, k_dec, f_last))
    return from_chunks(o)


def rwkv7_scan(r, decay, k, v, a_vec, b_vec):
    xs = tuple(jnp.swapaxes(t, 0, 1) for t in (r, decay, k, v, a_vec, b_vec))

    def step(S, inp):
        r_t, d_t, k_t, v_t, a_t, b_t = inp
        sa = jnp.einsum('bhvk,bhk->bhv', S, a_t)
        S = S * d_t[:, :, None, :] + sa[..., None] * b_t[:, :, None, :] + v_t[..., None] * k_t[:, :, None, :]
        return S, jnp.einsum('bhvk,bhk->bhv', S, r_t)

    bsz, _, hh, n = r.shape
    _, y = lax.scan(step, jnp.zeros((bsz, hh, n, n), jnp.float32), xs)
    return jnp.swapaxes(y, 0, 1)


def group_norm_heads(y, w, b):
    mean = jnp.mean(y, axis=-1, keepdims=True)
    var = jnp.mean(jnp.square(y - mean), axis=-1, keepdims=True)
    return (y - mean) * lax.rsqrt(var + GN_EPS) * w + b


def gated_deltanet(u, w_in, conv_w, A_log, dt_bias, norm_w, w_out):
    bsz, T, _ = u.shape
    f32 = jnp.float32
    proj = u @ w_in
    qkv = jax.nn.silu(causal_dwconv(proj[..., :QKV_A], conv_w)).astype(f32)
    z = proj[..., QKV_A:QKV_A + VAL_A].astype(f32)
    a_raw = proj[..., QKV_A + VAL_A:QKV_A + VAL_A + H_A].astype(f32)
    b_raw = proj[..., QKV_A + VAL_A + H_A:].astype(f32)
    q = l2norm(qkv[..., :KEY_A].reshape(bsz, T, H_A, DK_A)) * DK_A ** -0.5
    k = l2norm(qkv[..., KEY_A:2 * KEY_A].reshape(bsz, T, H_A, DK_A))
    v = qkv[..., 2 * KEY_A:].reshape(bsz, T, H_A, DV_A)
    beta = jax.nn.sigmoid(b_raw)
    g = -jnp.exp(A_log.astype(f32)) * jax.nn.softplus(a_raw + dt_bias.astype(f32))
    o = chunk_gated_delta_rule(q, k, v, beta, g)
    o = rms_norm(o, norm_w) * jax.nn.silu(z.reshape(bsz, T, H_A, DV_A))
    return o.reshape(bsz, T, VAL_A).astype(u.dtype) @ w_out


def hgrn2(u, w_in, lower_bound, norm_w, w_out):
    bsz, T, _ = u.shape
    f32 = jnp.float32
    proj = u @ w_in
    q = jax.nn.silu(proj[..., :KEY_B].astype(f32))
    f = lower_bound + (1.0 - lower_bound) * jax.nn.sigmoid(proj[..., KEY_B:2 * KEY_B].astype(f32))
    i_in = proj[..., 2 * KEY_B:2 * KEY_B + VAL_B].astype(f32)
    gate = proj[..., 2 * KEY_B + VAL_B:].astype(f32)
    kh = lambda t: t.reshape(bsz, T, H_B, DK_B)
    vh = lambda t: t.reshape(bsz, T, H_B, DV_B)
    o = chunk_gla(kh(q), kh(1.0 - f), vh(i_in), kh(jnp.log(f)))
    o = rms_norm(o, norm_w) * jax.nn.silu(vh(gate))
    return o.reshape(bsz, T, VAL_B).astype(u.dtype) @ w_out


def rwkv7_time_mix(u, mu, w_rkv, w0, w1, w2, a0, a1, a2, g1, g2, k_k, k_a, r_k, ln_w, ln_b, w_out):
    bsz, T, D = u.shape
    f32 = jnp.float32
    dx = jnp.pad(u, ((0, 0), (1, 0), (0, 0)))[:, :-1] - u
    xr, xw, xk, xv, xa, xg = (u + dx * mu[s] for s in range(6))
    r, k, v = jnp.einsum('sbtd,sde->sbte', jnp.stack([xr, xk, xv]), w_rkv).astype(f32)
    w = -jax.nn.softplus(-(w0 + jnp.tanh(xw @ w1) @ w2).astype(f32)) - 0.5
    decay = jnp.exp(-jnp.exp(w))
    a = jax.nn.sigmoid((a0 + (xa @ a1) @ a2).astype(f32))
    g = (jax.nn.sigmoid(xg @ g1) @ g2).astype(f32)
    heads = lambda t: t.reshape(bsz, T, H_C, HS_C)
    kk = l2norm(heads(k * k_k.astype(f32)))
    k = heads(k * (1.0 + (a - 1.0) * k_a.astype(f32)))
    r, v, a, decay = heads(r), heads(v), heads(a), heads(decay)
    y = rwkv7_scan(r, decay, k, v, -kk, kk * a)
    y = group_norm_heads(y, ln_w.reshape(H_C, HS_C).astype(f32), ln_b.reshape(H_C, HS_C).astype(f32))
    y = y + jnp.sum(r * k * r_k.astype(f32), axis=-1, keepdims=True) * v
    return (y.reshape(bsz, T, D) * g).astype(u.dtype) @ w_out


def conv_glu_ffn(u, w_up, conv_w, conv_b, w_down):
    hid = causal_dwconv(u @ w_up, conv_w) + conv_b
    val, gate = jnp.split(hid, 2, axis=-1)
    return (val * jax.nn.silu(gate)) @ w_down


def setup_inputs(seed: int = 0) -> dict:
    key = jax.random.key(seed)
    ks = iter(jax.random.split(key, 40))

    def nrm(shape, scale):
        return scale * jax.random.normal(next(ks), shape, jnp.float32)

    def unif(shape, lo, hi):
        return jax.random.uniform(next(ks), shape, jnp.float32, lo, hi)

    D = D_MODEL
    x = nrm((BATCH, SEQ, D), 1.0)
    c = nrm((BATCH, D), 1.0)
    norm_g = 1.0 + nrm((DEPTH, 2, D), 0.02)
    ada_w = nrm((DEPTH, D, 6 * D), 0.5 * D ** -0.5)
    ada_b = nrm((DEPTH, 6 * D), 0.02)
    dt = jnp.exp(unif((NL_A, H_A), math.log(1e-3), math.log(1e-1)))
    a_dt_bias = dt + jnp.log(-jnp.expm1(-dt))
    return {
        'x': x,
        'c': c,
        'norm_g': norm_g,
        'ada_w': ada_w,
        'ada_b': ada_b,
        'a_w_in': nrm((NL_A, D, IN_A), D ** -0.5),
        'a_conv': nrm((NL_A, CONV_A, QKV_A), CONV_A ** -0.5),
        'a_A_log': jnp.log(unif((NL_A, H_A), 1.0, 16.0)),
        'a_dt_bias': a_dt_bias,
        'a_norm': 1.0 + nrm((NL_A, DV_A), 0.02),
        'a_w_out': nrm((NL_A, VAL_A, D), VAL_A ** -0.5),
        'b_w_in': nrm((NL_B, D, IN_B), D ** -0.5),
        'hgrn_lb_logits': nrm((DEPTH, KEY_B), 0.1),
        'b_norm': 1.0 + nrm((NL_B, DV_B), 0.02),
        'b_w_out': nrm((NL_B, VAL_B, D), VAL_B ** -0.5),
        'c_mu': unif((NL_C, 6, D), 0.0, 1.0),
        'c_w_rkv': nrm((NL_C, 3, D, D), D ** -0.5),
        'c_w0': unif((NL_C, D), -6.5, -1.5),
        'c_w1': nrm((NL_C, D, LORA_W), D ** -0.5),
        'c_w2': nrm((NL_C, LORA_W, D), 0.1 * LORA_W ** -0.5),
        'c_a0': nrm((NL_C, D), 0.1),
        'c_a1': nrm((NL_C, D, LORA_A), D ** -0.5),
        'c_a2': nrm((NL_C, LORA_A, D), 0.1 * LORA_A ** -0.5),
        'c_g1': nrm((NL_C, D, LORA_G), D ** -0.5),
        'c_g2': nrm((NL_C, LORA_G, D), LORA_G ** -0.5),
        'c_k_k': 0.85 + nrm((NL_C, D), 0.02),
        'c_k_a': 1.0 + nrm((NL_C, D), 0.02),
        'c_r_k': nrm((NL_C, H_C, HS_C), 0.1),
        'c_ln_w': 1.0 + nrm((NL_C, D), 0.02),
        'c_ln_b': nrm((NL_C, D), 0.02),
        'c_w_out': nrm((NL_C, D, D), D ** -0.5),
        'f_w_up': nrm((DEPTH, D, 2 * D_FF), D ** -0.5),
        'f_conv_w': nrm((DEPTH, CONV_F, 2 * D_FF), CONV_F ** -0.5),
        'f_conv_b': nrm((DEPTH, 2 * D_FF), 0.02),
        'f_w_down': nrm((DEPTH, D_FF, D), D_FF ** -0.5),
        'final_g': 1.0 + nrm((D,), 0.02),
    }


def reference(x, c, norm_g, ada_w, ada_b, a_w_in, a_conv, a_A_log, a_dt_bias, a_norm, a_w_out,
              b_w_in, hgrn_lb_logits, b_norm, b_w_out,
              c_mu, c_w_rkv, c_w0, c_w1, c_w2, c_a0, c_a1, c_a2, c_g1, c_g2, c_k_k, c_k_a, c_r_k,
              c_ln_w, c_ln_b, c_w_out,
              f_w_up, f_conv_w, f_conv_b, f_w_down, final_g):
    bsz = x.shape[0]
    cond = jax.nn.silu(c)
    lb_p = jax.nn.softmax(hgrn_lb_logits.astype(jnp.float32), axis=0)
    lower_bounds = jnp.cumsum(lb_p, axis=0) - lb_p[0]
    h = x
    for i in range(DEPTH):
        mod = (cond @ ada_w[i] + ada_b[i]).reshape(bsz, 6, 1, D_MODEL)
        shift1, scale1, gate1, shift2, scale2, gate2 = (mod[:, s] for s in range(6))
        u = rms_norm(h, norm_g[i, 0]) * (1.0 + scale1) + shift1
        kind, j = i % N_MIXERS, i // N_MIXERS
        if kind == 0:
            y = gated_deltanet(u, a_w_in[j], a_conv[j], a_A_log[j], a_dt_bias[j], a_norm[j], a_w_out[j])
        elif kind == 1:
            y = hgrn2(u, b_w_in[j], lower_bounds[i], b_norm[j], b_w_out[j])
        else:
            y = rwkv7_time_mix(u, c_mu[j], c_w_rkv[j], c_w0[j], c_w1[j], c_w2[j], c_a0[j], c_a1[j], c_a2[j],
                               c_g1[j], c_g2[j], c_k_k[j], c_k_a[j], c_r_k[j], c_ln_w[j], c_ln_b[j], c_w_out[j])
        h = h + gate1 * y
        u = rms_norm(h, norm_g[i, 1]) * (1.0 + scale2) + shift2
        h = h + gate2 * conv_glu_ffn(u, f_w_up[i], f_conv_w[i], f_conv_b[i], f_w_down[i])
    return rms_norm(h, final_g)
```

```python
import functools
import math

import numpy as np
import jax
import jax.numpy as jnp
from jax import lax
from jax.experimental import pallas as pl
from jax.experimental.pallas import tpu as pltpu

F32 = jnp.float32
BF16 = jnp.bfloat16

CHUNK = 64
NORM_EPS = 1e-6
LANES = 128
SUBLANES = 8
VMEM_LIMIT_BYTES = 56 * 1024 * 1024


def _params(*sem):
    return pltpu.CompilerParams(dimension_semantics=sem, vmem_limit_bytes=VMEM_LIMIT_BYTES)


def _mm(a, b):
    return jnp.dot(a.astype(BF16), b.astype(BF16), preferred_element_type=F32)


def _mm_nt(a, b):
    return lax.dot_general(a.astype(BF16), b.astype(BF16), (((1,), (1,)), ((), ())),
                           preferred_element_type=F32)


def _mm_tn(a, b):
    return lax.dot_general(a.astype(BF16), b.astype(BF16), (((0,), (0,)), ((), ())),
                           preferred_element_type=F32)


def _split2(x):
    hi = x.astype(BF16)
    lo = (x - hi.astype(F32)).astype(BF16)
    return hi, lo


def _sigmoid(x):
    return 1.0 / (1.0 + jnp.exp(-x))


def _silu(x):
    return x * _sigmoid(x)


def _softplus(x):
    return jnp.maximum(x, 0.0) + jnp.log(1.0 + jnp.exp(-jnp.abs(x)))


def _norm_mod(h, g, scale, shift):
    ms = jnp.mean(h * h, axis=-1, keepdims=True)
    y = h * lax.rsqrt(ms + NORM_EPS)
    return (y * g) * (1.0 + scale) + shift


def _shift_rows(x, prev8, s):
    rolled = pltpu.roll(x, s, axis=0)
    head = pltpu.roll(prev8, s, axis=0)
    row = lax.broadcasted_iota(jnp.int32, (SUBLANES, x.shape[1]), 0)
    top = jnp.where(row < s, head, rolled[:SUBLANES])
    return jnp.concatenate([top, rolled[SUBLANES:]], axis=0)


def _mod_kernel(c_ref, w_ref, b_ref, o_ref):
    c = c_ref[...]
    o_ref[0] = _mm(_silu(c), w_ref[0]) + b_ref[0]


def _ada_mod(c, ada_w, ada_b):
    depth, d, d6 = ada_w.shape
    bsz = c.shape[0]
    rows = -(-bsz // SUBLANES) * SUBLANES
    c_pad = jnp.zeros((rows, d), F32).at[:bsz].set(c)
    out = pl.pallas_call(
        _mod_kernel,
        grid=(depth, d6 // d),
        in_specs=[pl.BlockSpec((rows, d), lambda i, j: (0, 0)),
                  pl.BlockSpec((1, d, d), lambda i, j: (i, 0, j)),
                  pl.BlockSpec((1, 1, d), lambda i, j: (i, 0, j))],
        out_specs=pl.BlockSpec((1, rows, d), lambda i, j: (i, 0, j)),
        out_shape=jax.ShapeDtypeStruct((depth, rows, d6), F32),
        compiler_params=_params("parallel", "parallel"),
        name="ada_mod",
    )(c_pad, ada_w, ada_b.reshape(depth, 1, d6))
    return out[:, :bsz].reshape(depth, bsz, d6 // d, d)


def _proj_kernel(h_ref, g_ref, sc_ref, sh_ref, w_ref, *rest, n_col_tiles, col_tile, has_small):
    if has_small:
        wst_ref, o_ref, ost_ref = rest
    else:
        (o_ref,) = rest
    u = _norm_mod(h_ref[0], g_ref[...], sc_ref[0], sh_ref[0])
    ub = u.astype(BF16)
    for j in range(n_col_tiles):
        sl = slice(j * col_tile, (j + 1) * col_tile)
        o_ref[0, :, sl] = jnp.dot(ub, w_ref[:, sl], preferred_element_type=F32).astype(o_ref.dtype)
    if has_small:
        u_hi, u_lo = _split2(u)
        w_hi, w_lo = _split2(wst_ref[...])
        ost_ref[0] = _mm_nt(w_hi, u_hi) + _mm_nt(w_hi, u_lo) + _mm_nt(w_lo, u_hi)


def _proj(h, g, scale, shift, w, w_small_t=None, *, row_tile=512, col_tile=1024, out_dtype=F32):
    bsz, t, d = h.shape
    n = w.shape[1]
    col_tile = min(col_tile, n)
    row_tile = min(row_tile, t)
    assert t % row_tile == 0 and n % col_tile == 0
    has_small = w_small_t is not None
    in_specs = [pl.BlockSpec((1, row_tile, d), lambda b, i: (b, i, 0)),
                pl.BlockSpec((1, d), lambda b, i: (0, 0)),
                pl.BlockSpec((1, 1, d), lambda b, i: (b, 0, 0)),
                pl.BlockSpec((1, 1, d), lambda b, i: (b, 0, 0)),
                pl.BlockSpec((d, n), lambda b, i: (0, 0))]
    args = [h, g.reshape(1, d), scale.reshape(bsz, 1, d), shift.reshape(bsz, 1, d), w]
    out_specs = [pl.BlockSpec((1, row_tile, n), lambda b, i: (b, i, 0))]
    out_shape = [jax.ShapeDtypeStruct((bsz, t, n), out_dtype)]
    if has_small:
        r = w_small_t.shape[0]
        in_specs.append(pl.BlockSpec((r, d), lambda b, i: (0, 0)))
        args.append(w_small_t)
        out_specs.append(pl.BlockSpec((1, r, row_tile), lambda b, i: (b, 0, i)))
        out_shape.append(jax.ShapeDtypeStruct((bsz, r, t), F32))
    outs = pl.pallas_call(
        functools.partial(_proj_kernel, n_col_tiles=n // col_tile, col_tile=col_tile, has_small=has_small),
        grid=(bsz, t // row_tile),
        in_specs=in_specs, out_specs=out_specs, out_shape=out_shape,
        compiler_params=_params("parallel", "parallel"),
        name="norm_proj",
    )(*args)
    return outs if has_small else outs[0]


def _out_kernel(x_ref, w_ref, h_ref, gate_ref, o_ref):
    y = jnp.dot(x_ref[0], w_ref[...], preferred_element_type=F32)
    o_ref[0] = h_ref[0] + gate_ref[0] * y


def _out_proj(x, w, h, gate, *, row_tile=1024):
    bsz, t, k = x.shape
    d = w.shape[1]
    row_tile = min(row_tile, t)
    assert t % row_tile == 0
    return pl.pallas_call(
        _out_kernel,
        grid=(bsz, t // row_tile),
        in_specs=[pl.BlockSpec((1, row_tile, k), lambda b, i: (b, i, 0)),
                  pl.BlockSpec((k, d), lambda b, i: (0, 0)),
                  pl.BlockSpec((1, row_tile, d), lambda b, i: (b, i, 0)),
                  pl.BlockSpec((1, 1, d), lambda b, i: (b, 0, 0))],
        out_specs=pl.BlockSpec((1, row_tile, d), lambda b, i: (b, i, 0)),
        out_shape=jax.ShapeDtypeStruct((bsz, t, d), F32),
        compiler_params=_params("parallel", "parallel"),
        name="out_proj",
    )(x, w, h, gate.reshape(bsz, 1, d))


def _ffn_kernel(h_ref, g_ref, sc_ref, sh_ref, gate_ref, wv_ref, wg_ref, cw_ref, cb_ref, wd_ref, fg_ref,
                o_ref, act_ref, carry_ref, *, d_ff, col_tile, final_norm):
    i = pl.program_id(1)
    h = h_ref[0]
    ub = _norm_mod(h, g_ref[...], sc_ref[0], sh_ref[0]).astype(BF16)

    @pl.when(i == 0)
    def _():
        carry_ref[...] = jnp.zeros_like(carry_ref)

    def conv(up, prev8, w3, bias):
        return (_shift_rows(up, prev8, 2) * w3[0:1] + _shift_rows(up, prev8, 1) * w3[1:2]
                + up * w3[2:3] + bias)

    for j in range(d_ff // col_tile):
        sl = slice(j * col_tile, (j + 1) * col_tile)
        sl_g = slice(d_ff + j * col_tile, d_ff + (j + 1) * col_tile)
        up_v = jnp.dot(ub, wv_ref[:, sl], preferred_element_type=F32)
        up_g = jnp.dot(ub, wg_ref[:, sl], preferred_element_type=F32)
        val = conv(up_v, carry_ref[:, sl], cw_ref[:, sl], cb_ref[:, sl])
        gat = conv(up_g, carry_ref[:, sl_g], cw_ref[:, sl_g], cb_ref[:, sl_g])
        carry_ref[:, sl] = up_v[-SUBLANES:]
        carry_ref[:, sl_g] = up_g[-SUBLANES:]
        act_ref[:, sl] = (val * _silu(gat)).astype(BF16)

    y = jnp.dot(act_ref[...], wd_ref[...], preferred_element_type=F32)
    out = h + gate_ref[0] * y
    if final_norm:
        ms = jnp.mean(out * out, axis=-1, keepdims=True)
        out = out * lax.rsqrt(ms + NORM_EPS) * fg_ref[...]
    o_ref[0] = out


def _ffn(h, g, scale, shift, gate, w_up, conv_w, conv_b, w_down, final_g=None, *, row_tile=256, col_tile=256):
    bsz, t, d = h.shape
    d_ff = w_down.shape[0]
    row_tile = min(row_tile, t)
    assert t % row_tile == 0 and d_ff % col_tile == 0
    final_norm = final_g is not None
    fg = (final_g if final_norm else jnp.ones((d,), F32)).reshape(1, d)
    wv = w_up[:, :d_ff].astype(BF16)
    wg = w_up[:, d_ff:].astype(BF16)
    const = lambda b, i: (0, 0)
    per_b = lambda b, i: (b, 0, 0)
    return pl.pallas_call(
        functools.partial(_ffn_kernel, d_ff=d_ff, col_tile=col_tile, final_norm=final_norm),
        grid=(bsz, t // row_tile),
        in_specs=[pl.BlockSpec((1, row_tile, d), lambda b, i: (b, i, 0)),
                  pl.BlockSpec((1, d), const),
                  pl.BlockSpec((1, 1, d), per_b),
                  pl.BlockSpec((1, 1, d), per_b),
                  pl.BlockSpec((1, 1, d), per_b),
                  pl.BlockSpec((d, d_ff), const),
                  pl.BlockSpec((d, d_ff), const),
                  pl.BlockSpec((3, 2 * d_ff), const),
                  pl.BlockSpec((1, 2 * d_ff), const),
                  pl.BlockSpec((d_ff, d), const),
                  pl.BlockSpec((1, d), const)],
        out_specs=pl.BlockSpec((1, row_tile, d), lambda b, i: (b, i, 0)),
        out_shape=jax.ShapeDtypeStruct((bsz, t, d), F32),
        scratch_shapes=[pltpu.VMEM((row_tile, d_ff), BF16),
                        pltpu.VMEM((SUBLANES, 2 * d_ff), F32)],
        compiler_params=_params("parallel", "arbitrary"),
        name="conv_glu_ffn",
    )(h, g.reshape(1, d), scale.reshape(bsz, 1, d), shift.reshape(bsz, 1, d), gate.reshape(bsz, 1, d),
      wv, wg, conv_w, conv_b.reshape(1, 2 * d_ff), w_down.astype(BF16), fg)


def _unit_lower_inverse(neg_l):
    c = neg_l.shape[0]
    ri = lax.broadcasted_iota(jnp.int32, (c, c), 0)
    ci = lax.broadcasted_iota(jnp.int32, (c, c), 1)
    p = jnp.where(ri == ci, 1.0, 0.0) + neg_l
    m = neg_l
    power = 1
    while 2 * power < c:
        m = _mm(m, m)
        p = p + _mm(p, m)
        power *= 2
    return p


def _head_rms_gate(o, norm_w, z):
    ms = jnp.mean(o * o, axis=-1, keepdims=True)
    return (o * lax.rsqrt(ms + NORM_EPS) * norm_w) * _silu(z)


def _dn_kernel(q_ref, k_ref, v_ref, z_ref, abt_ref, cwq_ref, cwk_ref, cwv_ref, alog_ref, dtb_ref, nw_ref,
               o_ref, qs_ref, ks_ref, vs_ref, carry_ref, s_ref, *, n_chunks, n_heads, dk):
    h_idx = pl.program_id(1)

    @pl.when(pl.program_id(2) == 0)
    def _():
        carry_ref[...] = jnp.zeros_like(carry_ref)
        s_ref[...] = jnp.zeros_like(s_ref)

    def conv_silu(x_ref, slot, w_ref):
        x = x_ref[0]
        prev8 = carry_ref[slot]
        w = w_ref[...]
        y = (_shift_rows(x, prev8, 3) * w[0:1] + _shift_rows(x, prev8, 2) * w[1:2]
             + _shift_rows(x, prev8, 1) * w[2:3] + x * w[3:4])
        carry_ref[slot] = x[-SUBLANES:]
        return _silu(y)

    qc = conv_silu(q_ref, 0, cwq_ref)
    kc = conv_silu(k_ref, 1, cwk_ref)
    vs_ref[...] = conv_silu(v_ref, 2, cwv_ref)
    qs_ref[...] = qc * lax.rsqrt(jnp.sum(qc * qc, axis=-1, keepdims=True) + 1e-6) * (dk ** -0.5)
    ks_ref[...] = kc * lax.rsqrt(jnp.sum(kc * kc, axis=-1, keepdims=True) + 1e-6)

    a_row = abt_ref[0, pl.ds(h_idx, 1), :]
    b_row = abt_ref[0, pl.ds(n_heads + h_idx, 1), :]
    g_all = -jnp.exp(alog_ref[0][:, :1]) * _softplus(a_row + dtb_ref[0][:, :1])
    beta_all = _sigmoid(b_row)

    ri = lax.broadcasted_iota(jnp.int32, (CHUNK, CHUNK), 0)
    ci = lax.broadcasted_iota(jnp.int32, (CHUNK, CHUNK), 1)
    nw = nw_ref[...]

    for c in range(n_chunks):
        rows = pl.ds(c * CHUNK, CHUNK)
        lanes = slice(c * CHUNK, (c + 1) * CHUNK)
        q = qs_ref[rows, :]
        k = ks_ref[rows, :]
        v = vs_ref[rows, :]
        g_row = g_all[:, lanes]
        g_col = jnp.sum(jnp.where(ci <= ri, g_row, 0.0), axis=1, keepdims=True)
        g_rowc = jnp.sum(jnp.where(ci == ri, g_col, 0.0), axis=0, keepdims=True)
        beta_col = jnp.sum(jnp.where(ci == ri, beta_all[:, lanes], 0.0), axis=1, keepdims=True)
        g_last = g_col[CHUNK - 1:CHUNK, :]
        decay = jnp.where(ci <= ri, jnp.exp(jnp.minimum(g_col - g_rowc, 0.0)), 0.0)
        e_g = jnp.exp(g_col)
        kb = k * beta_col
        neg_l = jnp.where(ci < ri, -(_mm_nt(kb, k) * decay), 0.0)
        t_inv = _unit_lower_inverse(neg_l)
        w = _mm(t_inv, kb * e_g)
        u = _mm(t_inv, v * beta_col)
        attn = _mm_nt(q, k) * decay
        s = s_ref[...]
        v_new = u - _mm(w, s)
        o = _mm(q * e_g, s) + _mm(attn, v_new)
        s_ref[...] = s * jnp.exp(g_last) + _mm_tn(k * jnp.exp(g_last - g_col), v_new)
        o_ref[0, rows, :] = _head_rms_gate(o, nw, z_ref[0, rows, :]).astype(o_ref.dtype)


def _deltanet_core(proj, abt, conv_w, a_log, dt_bias, norm_w, *, time_tile=512):
    bsz, t, _ = proj.shape
    n_heads = a_log.shape[0]
    dv = norm_w.shape[0]
    time_tile = min(time_tile, t)
    assert dv == LANES and proj.shape[2] == 4 * n_heads * LANES and t % time_tile == 0
    hh = n_heads
    blk = lambda off: pl.BlockSpec((1, time_tile, LANES), lambda b, h, i: (b, i, off + h))
    cblk = lambda off: pl.BlockSpec((conv_w.shape[0], LANES), lambda b, h, i: (0, off + h))
    per_head = pl.BlockSpec((1, 1, LANES), lambda b, h, i: (h, 0, 0))
    bcast = lambda a: jnp.broadcast_to(a.astype(F32)[:, None, None], (n_heads, 1, LANES))
    return pl.pallas_call(
        functools.partial(_dn_kernel, n_chunks=time_tile // CHUNK, n_heads=n_heads, dk=LANES),
        grid=(bsz, n_heads, t // time_tile),
        in_specs=[blk(0), blk(hh), blk(2 * hh), blk(3 * hh),
                  pl.BlockSpec((1, 2 * n_heads, time_tile), lambda b, h, i: (b, 0, i)),
                  cblk(0), cblk(hh), cblk(2 * hh),
                  per_head, per_head,
                  pl.BlockSpec((1, LANES), lambda b, h, i: (0, 0))],
        out_specs=pl.BlockSpec((1, time_tile, LANES), lambda b, h, i: (b, i, h)),
        out_shape=jax.ShapeDtypeStruct((bsz, t, n_heads * dv), BF16),
        scratch_shapes=[pltpu.VMEM((time_tile, LANES), F32),
                        pltpu.VMEM((time_tile, LANES), F32),
                        pltpu.VMEM((time_tile, LANES), F32),
                        pltpu.VMEM((3, SUBLANES, LANES), F32),
                        pltpu.VMEM((LANES, LANES), F32)],
        compiler_params=_params("parallel", "parallel", "arbitrary"),
        name="deltanet_chunk",
    )(proj, proj, proj, proj, abt, conv_w, conv_w, conv_w, bcast(a_log), bcast(dt_bias), norm_w.reshape(1, dv))


def _deltanet_layer(h, g, scale, shift, gate, w_in, conv_w, a_log, dt_bias, norm_w, w_out):
    n_heads = a_log.shape[0]
    n_main = w_in.shape[1] - 2 * n_heads
    proj, abt = _proj(h, g, scale, shift, w_in[:, :n_main].astype(BF16), w_in[:, n_main:].T)
    og = _deltanet_core(proj, abt, conv_w, a_log, dt_bias, norm_w)
    return _out_proj(og, w_out.astype(BF16), h, gate)


def _gla_levels():
    levels, s = [], CHUNK // 2
    while s >= 1:
        levels.append(s)
        s //= 2
    return levels


def _gla_sum_matrix():
    c = CHUNK
    mats = [np.tril(np.ones((c, c), np.float32))]
    for s in _gla_levels():
        m_l = np.zeros((c, c), np.float32)
        for i in range(c):
            mid = (i // (2 * s)) * 2 * s + s
            if i >= mid:
                m_l[i, mid:i + 1] = 1.0
            else:
                m_l[i, i + 1:mid] = 1.0
        mats.append(m_l)
    return np.concatenate(mats, axis=0)


def _gla_kernel(q_ref, f_ref, i_ref, g_ref, lbl_ref, nw_ref, sm_ref, o_ref, st_ref, *, n_chunks, layer):
    @pl.when(pl.program_id(2) == 0)
    def _():
        st_ref[...] = jnp.zeros_like(st_ref)

    logits = lbl_ref[...]
    e = jnp.exp(logits - jnp.max(logits, axis=0, keepdims=True))
    if layer == 0:
        lb = jnp.zeros((1, LANES), F32)
    else:
        lb = jnp.sum(e[1:layer + 1], axis=0, keepdims=True) / jnp.sum(e, axis=0, keepdims=True)
    nw = nw_ref[...]
    sm = sm_ref[...]
    ri = lax.broadcasted_iota(jnp.int32, (CHUNK, CHUNK), 0)
    ci = lax.broadcasted_iota(jnp.int32, (CHUNK, CHUNK), 1)
    row = lax.broadcasted_iota(jnp.int32, (CHUNK, 1), 0)
    levels = _gla_levels()

    for c in range(n_chunks):
        rows = pl.ds(c * CHUNK, CHUNK)
        q = _silu(q_ref[0, rows, :])
        f = lb + (1.0 - lb) * _sigmoid(f_ref[0, rows, :])
        k = 1.0 - f
        v = i_ref[0, rows, :]
        lf_hi, lf_lo = _split2(jnp.log(f))
        sums = (jnp.dot(sm, lf_hi, preferred_element_type=F32)
                + jnp.dot(sm, lf_lo, preferred_element_type=F32))
        b = sums[:CHUNK]
        b_last = b[CHUNK - 1:CHUNK, :]
        a = jnp.where(ri == ci, _mm_nt(q, k), 0.0)
        for l, s in enumerate(levels):
            ex = jnp.exp(sums[(l + 1) * CHUNK:(l + 2) * CHUNK])
            upper = (row & (2 * s - 1)) >= s
            q_l = jnp.where(upper, q * ex, 0.0)
            k_l = jnp.where(upper, 0.0, k * ex)
            same = (ri // (2 * s)) == (ci // (2 * s))
            a = a + jnp.where(same, _mm_nt(q_l, k_l), 0.0)
        st = st_ref[...]
        o = _mm_nt(q * jnp.exp(b), st) + _mm(a, v)
        st_ref[...] = st * jnp.exp(b_last) + _mm_tn(v, k * jnp.exp(b_last - b))
        o_ref[0, rows, :] = _head_rms_gate(o, nw, g_ref[0, rows, :]).astype(o_ref.dtype)


def _gla_core(proj, lb_logits, layer, norm_w, *, time_tile=512):
    bsz, t, n = proj.shape
    dv = norm_w.shape[0]
    n_heads = n // (4 * LANES)
    time_tile = min(time_tile, t)
    assert dv == LANES and n == 4 * n_heads * LANES and t % time_tile == 0
    blk = lambda off: pl.BlockSpec((1, time_tile, LANES), lambda b, h, i: (b, i, off + h))
    sm = jnp.asarray(_gla_sum_matrix(), BF16)
    return pl.pallas_call(
        functools.partial(_gla_kernel, n_chunks=time_tile // CHUNK, layer=layer),
        grid=(bsz, n_heads, t // time_tile),
        in_specs=[blk(0), blk(n_heads), blk(2 * n_heads), blk(3 * n_heads),
                  pl.BlockSpec((lb_logits.shape[0], LANES), lambda b, h, i: (0, h)),
                  pl.BlockSpec((1, LANES), lambda b, h, i: (0, 0)),
                  pl.BlockSpec(sm.shape, lambda b, h, i: (0, 0))],
        out_specs=pl.BlockSpec((1, time_tile, LANES), lambda b, h, i: (b, i, h)),
        out_shape=jax.ShapeDtypeStruct((bsz, t, n_heads * dv), BF16),
        scratch_shapes=[pltpu.VMEM((LANES, LANES), F32)],
        compiler_params=_params("parallel", "parallel", "arbitrary"),
        name="gla_chunk",
    )(proj, proj, proj, proj, lb_logits.astype(F32), norm_w.reshape(1, dv), sm)


def _hgrn2_layer(h, g, scale, shift, gate, w_in, lb_logits, layer, norm_w, w_out):
    proj = _proj(h, g, scale, shift, w_in.astype(BF16))
    og = _gla_core(proj, lb_logits, layer, norm_w)
    return _out_proj(og, w_out.astype(BF16), h, gate)


def _rwkv_proj_kernel(h_ref, g_ref, sc_ref, sh_ref, mu_ref, wrkv_ref, w0_ref, w1_ref, w2_ref, a0_ref, a1_ref,
                      a2_ref, g1_ref, g2_ref, r_ref, k_ref, v_ref, a_ref, wl_ref, gate_ref, carry_ref):
    @pl.when(pl.program_id(1) == 0)
    def _():
        carry_ref[...] = jnp.zeros_like(carry_ref)

    u = _norm_mod(h_ref[0], g_ref[...], sc_ref[0], sh_ref[0])
    dx = _shift_rows(u, carry_ref[...], 1) - u
    carry_ref[...] = u[-SUBLANES:]
    mu = mu_ref[...]
    mix = lambda s: (u + dx * mu[s:s + 1]).astype(BF16)
    dot = lambda a, b: jnp.dot(a, b, preferred_element_type=F32)
    r_ref[0] = dot(mix(0), wrkv_ref[0])
    k_ref[0] = dot(mix(2), wrkv_ref[1])
    v_ref[0] = dot(mix(3), wrkv_ref[2])
    w = w0_ref[...] + dot(jnp.tanh(dot(mix(1), w1_ref[...])).astype(BF16), w2_ref[...])
    w = -_softplus(-w) - 0.5
    wl_ref[0] = -jnp.exp(w)
    a_ref[0] = _sigmoid(a0_ref[...] + dot(dot(mix(4), a1_ref[...]).astype(BF16), a2_ref[...]))
    gate_ref[0] = dot(_sigmoid(dot(mix(5), g1_ref[...])).astype(BF16), g2_ref[...]).astype(gate_ref.dtype)


def _pad_lora(w_in, w_out):
    rank = w_in.shape[1]
    pad = -(-rank // LANES) * LANES - rank
    return (jnp.pad(w_in, ((0, 0), (0, pad))).astype(BF16), jnp.pad(w_out, ((0, pad), (0, 0))).astype(BF16))


def _rwkv_proj(h, g, scale, shift, mu, w_rkv, w0, w1, w2, a0, a1, a2, g1, g2, *, row_tile=256):
    bsz, t, d = h.shape
    row_tile = min(row_tile, t)
    assert t % row_tile == 0
    w1p, w2p = _pad_lora(w1, w2)
    a1p, a2p = _pad_lora(a1, a2)
    g1p, g2p = _pad_lora(g1, g2)
    const2 = lambda b, i: (0, 0)
    per_b = lambda b, i: (b, 0, 0)
    tile = pl.BlockSpec((1, row_tile, d), lambda b, i: (b, i, 0))
    full = lambda a: pl.BlockSpec(a.shape, const2)
    vec = pl.BlockSpec((1, d), const2)
    f32_out = jax.ShapeDtypeStruct((bsz, t, d), F32)
    return pl.pallas_call(
        _rwkv_proj_kernel,
        grid=(bsz, t // row_tile),
        in_specs=[tile, vec, pl.BlockSpec((1, 1, d), per_b), pl.BlockSpec((1, 1, d), per_b),
                  full(mu), pl.BlockSpec(w_rkv.shape, lambda b, i: (0, 0, 0)),
                  vec, full(w1p), full(w2p), vec, full(a1p), full(a2p), full(g1p), full(g2p)],
        out_specs=[tile] * 6,
        out_shape=[f32_out] * 5 + [jax.ShapeDtypeStruct((bsz, t, d), BF16)],
        scratch_shapes=[pltpu.VMEM((SUBLANES, d), F32)],
        compiler_params=_params("parallel", "arbitrary"),
        name="rwkv_proj",
    )(h, g.reshape(1, d), scale.reshape(bsz, 1, d), shift.reshape(bsz, 1, d), mu, w_rkv.astype(BF16),
      w0.reshape(1, d), w1p, w2p, a0.reshape(1, d), a1p, a2p, g1p, g2p)


def _rwkv_chunk_kernel(r_ref, k_ref, v_ref, a_ref, wl_ref, gate_ref, kk_ref, ka_ref, rk_ref, lnw_ref, lnb_ref,
                       tri_ref, o_ref, s_ref, *, n_chunks, head_size, gn_eps):
    @pl.when(pl.program_id(2) == 0)
    def _():
        s_ref[...] = jnp.zeros_like(s_ref)

    c_len = CHUNK
    lane = lax.broadcasted_iota(jnp.int32, (1, LANES), 1)
    lo = lane < head_size
    lane_t = lax.broadcasted_iota(jnp.int32, (c_len, LANES), 1) & (head_size - 1)
    row_t = lax.broadcasted_iota(jnp.int32, (c_len, LANES), 0)
    strict = lane_t < row_t
    incl = lane_t <= row_t
    eye2 = jnp.where(lane_t == row_t, 1.0, 0.0)
    bd_r = lax.broadcasted_iota(jnp.int32, (LANES, LANES), 0) < head_size
    bd_c = lax.broadcasted_iota(jnp.int32, (LANES, LANES), 1) < head_size
    block_diag = bd_r == bd_c
    tri = tri_ref[...]

    def pair_sum(x):
        s_lo = jnp.sum(jnp.where(lo, x, 0.0), axis=-1, keepdims=True)
        s_hi = jnp.sum(jnp.where(lo, 0.0, x), axis=-1, keepdims=True)
        return jnp.where(lo, s_lo, s_hi)

    def stack(x, first_lo=True):
        x_lo, x_hi = jnp.where(lo, x, 0.0), jnp.where(lo, 0.0, x)
        return jnp.concatenate([x_lo, x_hi] if first_lo else [x_hi, x_lo], axis=0)

    for c in range(n_chunks):
        rows = pl.ds(c * c_len, c_len)
        r = r_ref[0, rows, :]
        k_raw = k_ref[0, rows, :]
        v = v_ref[0, rows, :]
        a = a_ref[0, rows, :]
        wl = wl_ref[0, rows, :]
        kx = k_raw * kk_ref[...]
        kk = kx * lax.rsqrt(pair_sum(kx * kx) + 1e-6)
        k = k_raw * (1.0 + (a - 1.0) * ka_ref[...])
        wl_hi, wl_lo = _split2(wl)
        cum = jnp.dot(tri, wl_hi, preferred_element_type=F32) + jnp.dot(tri, wl_lo, preferred_element_type=F32)
        inv_gam = jnp.exp(-cum)
        a_t = -kk * jnp.exp(cum - wl)
        r_t = r * jnp.exp(cum)
        b_t = kk * a * inv_gam
        k_t = k * inv_gam
        gam_last = jnp.exp(cum[c_len - 1:c_len, :])

        ar = jnp.concatenate([a_t, r_t], axis=0)
        r1 = _mm_nt(jnp.where(lo, ar, 0.0), jnp.concatenate([b_t, k_t], axis=0))
        r2 = _mm_nt(jnp.where(lo, 0.0, ar), jnp.concatenate([k_t, b_t], axis=0))
        a_ab = jnp.where(strict, jnp.where(lo, r1[:c_len], r2[:c_len]), 0.0)
        a_ak = jnp.where(strict, jnp.where(lo, r2[:c_len], r1[:c_len]), 0.0)
        a_rb = jnp.where(incl, jnp.where(lo, r1[c_len:], r2[c_len:]), 0.0)
        a_rk = jnp.where(incl, jnp.where(lo, r2[c_len:], r1[c_len:]), 0.0)

        t_inv = eye2 + a_ab
        m = a_ab
        power = 1
        while 2 * power < c_len:
            m = _mm(m, stack(m))
            t_inv = t_inv + _mm(t_inv, stack(m))
            power *= 2

        w_t = _mm(t_inv, stack(a_t))
        u_0 = _mm(t_inv, stack(_mm(a_ak, stack(v, first_lo=False))))
        s = s_ref[...]
        u = _mm_nt(w_t, s) + u_0
        y = _mm_nt(r_t, s) + _mm(jnp.concatenate([a_rb, a_rk], axis=1),
                                 jnp.concatenate([stack(u), stack(v, first_lo=False)], axis=0))
        upd = _mm_tn(jnp.concatenate([u, v], axis=0), jnp.concatenate([b_t, k_t], axis=0))
        s_ref[...] = (s + jnp.where(block_diag, upd, 0.0)) * gam_last

        inv_n = 1.0 / head_size
        mean = pair_sum(y) * inv_n
        yc = y - mean
        var = pair_sum(yc * yc) * inv_n
        y_n = yc * lax.rsqrt(var + gn_eps) * lnw_ref[...] + lnb_ref[...]
        bonus = pair_sum(r * k * rk_ref[...]) * v
        o_ref[0, rows, :] = ((y_n + bonus) * gate_ref[0, rows, :].astype(F32)).astype(o_ref.dtype)


def _rwkv_chunk(r, k, v, a, wl, gate, k_k, k_a, r_k, ln_w, ln_b, *, time_tile=512):
    bsz, t, d = r.shape
    head_size = r_k.shape[1]
    assert 2 * head_size == LANES and d % LANES == 0
    time_tile = min(time_tile, t)
    assert t % time_tile == 0
    n_pairs = d // LANES
    blk = pl.BlockSpec((1, time_tile, LANES), lambda b, p, i: (b, i, p))
    vec = pl.BlockSpec((1, LANES), lambda b, p, i: (0, p))
    tri = jnp.asarray(np.tril(np.ones((CHUNK, CHUNK), np.float32)), BF16)
    flat = lambda x: x.reshape(1, d).astype(F32)
    return pl.pallas_call(
        functools.partial(_rwkv_chunk_kernel, n_chunks=time_tile // CHUNK, head_size=head_size,
                          gn_eps=1e-5 * head_size * head_size),
        grid=(bsz, n_pairs, t // time_tile),
        in_specs=[blk] * 6 + [vec] * 5 + [pl.BlockSpec((CHUNK, CHUNK), lambda b, p, i: (0, 0))],
        out_specs=blk,
        out_shape=jax.ShapeDtypeStruct((bsz, t, d), BF16),
        scratch_shapes=[pltpu.VMEM((LANES, LANES), F32)],
        compiler_params=_params("parallel", "parallel", "arbitrary"),
        name="rwkv_chunk",
    )(r, k, v, a, wl, gate, flat(k_k), flat(k_a), flat(r_k), flat(ln_w), flat(ln_b), tri)


def _rwkv7_layer(h, g, scale, shift, gate, mu, w_rkv, w0, w1, w2, a0, a1, a2, g1, g2, k_k, k_a, r_k, ln_w, ln_b,
                 w_out):
    r, k, v, a, wl, gt = _rwkv_proj(h, g, scale, shift, mu, w_rkv, w0, w1, w2, a0, a1, a2, g1, g2)
    yg = _rwkv_chunk(r, k, v, a, wl, gt, k_k, k_a, r_k, ln_w, ln_b)
    return _out_proj(yg, w_out.astype(BF16), h, gate)


def kernel(x, c, norm_g, ada_w, ada_b, a_w_in, a_conv, a_A_log, a_dt_bias, a_norm, a_w_out, b_w_in, hgrn_lb_logits, b_norm, b_w_out, c_mu, c_w_rkv, c_w0, c_w1, c_w2, c_a0, c_a1, c_a2, c_g1, c_g2, c_k_k, c_k_a, c_r_k, c_ln_w, c_ln_b, c_w_out, f_w_up, f_conv_w, f_conv_b, f_w_down, final_g):
    depth = ada_w.shape[0]
    n_mixers = 3
    mod = _ada_mod(c, ada_w, ada_b)
    h = x
    for i in range(depth):
        shift1, scale1, gate1, shift2, scale2, gate2 = (mod[i, :, s] for s in range(6))
        kind, j = i % n_mixers, i // n_mixers
        if kind == 0:
            h = _deltanet_layer(h, norm_g[i, 0], scale1, shift1, gate1, a_w_in[j], a_conv[j], a_A_log[j],
                                a_dt_bias[j], a_norm[j], a_w_out[j])
        elif kind == 1:
            h = _hgrn2_layer(h, norm_g[i, 0], scale1, shift1, gate1, b_w_in[j], hgrn_lb_logits, i, b_norm[j],
                             b_w_out[j])
        else:
            h = _rwkv7_layer(h, norm_g[i, 0], scale1, shift1, gate1, c_mu[j], c_w_rkv[j], c_w0[j], c_w1[j],
                             c_w2[j], c_a0[j], c_a1[j], c_a2[j], c_g1[j], c_g2[j], c_k_k[j], c_k_a[j], c_r_k[j],
                             c_ln_w[j], c_ln_b[j], c_w_out[j])
        h = _ffn(h, norm_g[i, 1], scale2, shift2, gate2, f_w_up[i], f_conv_w[i], f_conv_b[i], f_w_down[i],
                 final_g if i == depth - 1 else None)
    return h
```

```python
import functools
import math

import numpy as np
import jax
import jax.numpy as jnp
from jax import lax
from jax.experimental import pallas as pl
from jax.experimental.pallas import tpu as pltpu

F32 = jnp.float32
BF16 = jnp.bfloat16

CHUNK = 64
NORM_EPS = 1e-6
LANES = 128
SUBLANES = 8
VMEM_LIMIT_BYTES = 56 * 1024 * 1024


def _params(*sem):
    return pltpu.CompilerParams(dimension_semantics=sem, vmem_limit_bytes=VMEM_LIMIT_BYTES)


def _mm(a, b):
    return jnp.dot(a.astype(BF16), b.astype(BF16), preferred_element_type=F32)


def _mm_nt(a, b):
    return lax.dot_general(a.astype(BF16), b.astype(BF16), (((1,), (1,)), ((), ())),
                           preferred_element_type=F32)


def _mm_tn(a, b):
    return lax.dot_general(a.astype(BF16), b.astype(BF16), (((0,), (0,)), ((), ())),
                           preferred_element_type=F32)


def _split2(x):
    hi = x.astype(BF16)
    lo = (x - hi.astype(F32)).astype(BF16)
    return hi, lo


def _sigmoid(x):
    return 1.0 / (1.0 + jnp.exp(-x))


def _silu(x):
    return x * _sigmoid(x)


def _softplus(x):
    return jnp.maximum(x, 0.0) + jnp.log(1.0 + jnp.exp(-jnp.abs(x)))


def _norm_mod(h, g, scale, shift):
    ms = jnp.mean(h * h, axis=-1, keepdims=True)
    y = h * lax.rsqrt(ms + NORM_EPS)
    return (y * g) * (1.0 + scale) + shift


def _shift_rows(x, prev8, s):
    rolled = pltpu.roll(x, s, axis=0)
    head = pltpu.roll(prev8, s, axis=0)
    row = lax.broadcasted_iota(jnp.int32, (SUBLANES, x.shape[1]), 0)
    top = jnp.where(row < s, head, rolled[:SUBLANES])
    return jnp.concatenate([top, rolled[SUBLANES:]], axis=0)


def _mod_kernel(c_ref, w_ref, b_ref, o_ref):
    c = c_ref[...]
    o_ref[0] = _mm(_silu(c), w_ref[0]) + b_ref[0]


def _ada_mod(c, ada_w, ada_b):
    depth, d, d6 = ada_w.shape
    bsz = c.shape[0]
    rows = -(-bsz // SUBLANES) * SUBLANES
    c_pad = jnp.zeros((rows, d), F32).at[:bsz].set(c)
    out = pl.pallas_call(
        _mod_kernel,
        grid=(depth, d6 // d),
        in_specs=[pl.BlockSpec((rows, d), lambda i, j: (0, 0)),
                  pl.BlockSpec((1, d, d), lambda i, j: (i, 0, j)),
                  pl.BlockSpec((1, 1, d), lambda i, j: (i, 0, j))],
        out_specs=pl.BlockSpec((1, rows, d), lambda i, j: (i, 0, j)),
        out_shape=jax.ShapeDtypeStruct((depth, rows, d6), F32),
        compiler_params=_params("parallel", "parallel"),
        name="ada_mod",
    )(c_pad, ada_w, ada_b.reshape(depth, 1, d6))
    return out[:, :bsz].reshape(depth, bsz, d6 // d, d)


def _proj_kernel(h_ref, g_ref, sc_ref, sh_ref, w_ref, *rest, n_col_tiles, col_tile, has_small):
    if has_small:
        wst_ref, o_ref, ost_ref = rest
    else:
        (o_ref,) = rest
    u = _norm_mod(h_ref[0], g_ref[...], sc_ref[0], sh_ref[0])
    ub = u.astype(BF16)
    for j in range(n_col_tiles):
        sl = slice(j * col_tile, (j + 1) * col_tile)
        o_ref[0, :, sl] = jnp.dot(ub, w_ref[:, sl], preferred_element_type=F32).astype(o_ref.dtype)
    if has_small:
        u_hi, u_lo = _split2(u)
        w_hi, w_lo = _split2(wst_ref[...])
        ost_ref[0] = _mm_nt(w_hi, u_hi) + _mm_nt(w_hi, u_lo) + _mm_nt(w_lo, u_hi)


def _proj(h, g, scale, shift, w, w_small_t=None, *, row_tile=512, col_tile=1024, out_dtype=F32):
    bsz, t, d = h.shape
    n = w.shape[1]
    col_tile = min(col_tile, n)
    row_tile = min(row_tile, t)
    assert t % row_tile == 0 and n % col_tile == 0
    has_small = w_small_t is not None
    in_specs = [pl.BlockSpec((1, row_tile, d), lambda b, i: (b, i, 0)),
                pl.BlockSpec((1, d), lambda b, i: (0, 0)),
                pl.BlockSpec((1, 1, d), lambda b, i: (b, 0, 0)),
                pl.BlockSpec((1, 1, d), lambda b, i: (b, 0, 0)),
                pl.BlockSpec((d, n), lambda b, i: (0, 0))]
    args = [h, g.reshape(1, d), scale.reshape(bsz, 1, d), shift.reshape(bsz, 1, d), w]
    out_specs = [pl.BlockSpec((1, row_tile, n), lambda b, i: (b, i, 0))]
    out_shape = [jax.ShapeDtypeStruct((bsz, t, n), out_dtype)]
    if has_small:
        r = w_small_t.shape[0]
        in_specs.append(pl.BlockSpec((r, d), lambda b, i: (0, 0)))
        args.append(w_small_t)
        out_specs.append(pl.BlockSpec((1, r, row_tile), lambda b, i: (b, 0, i)))
        out_shape.append(jax.ShapeDtypeStruct((bsz, r, t), F32))
    outs = pl.pallas_call(
        functools.partial(_proj_kernel, n_col_tiles=n // col_tile, col_tile=col_tile, has_small=has_small),
        grid=(bsz, t // row_tile),
        in_specs=in_specs, out_specs=out_specs, out_shape=out_shape,
        compiler_params=_params("parallel", "parallel"),
        name="norm_proj",
    )(*args)
    return outs if has_small else outs[0]


def _out_kernel(x_ref, w_ref, h_ref, gate_ref, o_ref):
    y = jnp.dot(x_ref[0], w_ref[...], preferred_element_type=F32)
    o_ref[0] = h_ref[0] + gate_ref[0] * y


def _out_proj(x, w, h, gate, *, row_tile=1024):
    bsz, t, k = x.shape
    d = w.shape[1]
    row_tile = min(row_tile, t)
    assert t % row_tile == 0
    return pl.pallas_call(
        _out_kernel,
        grid=(bsz, t // row_tile),
        in_specs=[pl.BlockSpec((1, row_tile, k), lambda b, i: (b, i, 0)),
                  pl.BlockSpec((k, d), lambda b, i: (0, 0)),
                  pl.BlockSpec((1, row_tile, d), lambda b, i: (b, i, 0)),
                  pl.BlockSpec((1, 1, d), lambda b, i: (b, 0, 0))],
        out_specs=pl.BlockSpec((1, row_tile, d), lambda b, i: (b, i, 0)),
        out_shape=jax.ShapeDtypeStruct((bsz, t, d), F32),
        compiler_params=_params("parallel", "parallel"),
        name="out_proj",
    )(x, w, h, gate.reshape(bsz, 1, d))


def _ffn_kernel(h_ref, g_ref, sc_ref, sh_ref, gate_ref, wv_ref, wg_ref, cw_ref, cb_ref, wd_ref, fg_ref,
                o_ref, act_ref, carry_ref, *, d_ff, col_tile, final_norm):
    i = pl.program_id(1)
    h = h_ref[0]
    ub = _norm_mod(h, g_ref[...], sc_ref[0], sh_ref[0]).astype(BF16)

    @pl.when(i == 0)
    def _():
        carry_ref[...] = jnp.zeros_like(carry_ref)

    def conv(up, prev8, w3, bias):
        return (_shift_rows(up, prev8, 2) * w3[0:1] + _shift_rows(up, prev8, 1) * w3[1:2]
                + up * w3[2:3] + bias)

    for j in range(d_ff // col_tile):
        sl = slice(j * col_tile, (j + 1) * col_tile)
        sl_g = slice(d_ff + j * col_tile, d_ff + (j + 1) * col_tile)
        up_v = jnp.dot(ub, wv_ref[:, sl], preferred_element_type=F32)
        up_g = jnp.dot(ub, wg_ref[:, sl], preferred_element_type=F32)
        val = conv(up_v, carry_ref[:, sl], cw_ref[:, sl], cb_ref[:, sl])
        gat = conv(up_g, carry_ref[:, sl_g], cw_ref[:, sl_g], cb_ref[:, sl_g])
        carry_ref[:, sl] = up_v[-SUBLANES:]
        carry_ref[:, sl_g] = up_g[-SUBLANES:]
        act_ref[:, sl] = (val * _silu(gat)).astype(BF16)

    y = jnp.dot(act_ref[...], wd_ref[...], preferred_element_type=F32)
    out = h + gate_ref[0] * y
    if final_norm:
        ms = jnp.mean(out * out, axis=-1, keepdims=True)
        out = out * lax.rsqrt(ms + NORM_EPS) * fg_ref[...]
    o_ref[0] = out


def _ffn(h, g, scale, shift, gate, w_up, conv_w, conv_b, w_down, final_g=None, *, row_tile=256, col_tile=256):
    bsz, t, d = h.shape
    d_ff = w_down.shape[0]
    row_tile = min(row_tile, t)
    assert t % row_tile == 0 and d_ff % col_tile == 0
    final_norm = final_g is not None
    fg = (final_g if final_norm else jnp.ones((d,), F32)).reshape(1, d)
    wv = w_up[:, :d_ff].astype(BF16)
    wg = w_up[:, d_ff:].astype(BF16)
    const = lambda b, i: (0, 0)
    per_b = lambda b, i: (b, 0, 0)
    return pl.pallas_call(
        functools.partial(_ffn_kernel, d_ff=d_ff, col_tile=col_tile, final_norm=final_norm),
        grid=(bsz, t // row_tile),
        in_specs=[pl.BlockSpec((1, row_tile, d), lambda b, i: (b, i, 0)),
                  pl.BlockSpec((1, d), const),
                  pl.BlockSpec((1, 1, d), per_b),
                  pl.BlockSpec((1, 1, d), per_b),
                  pl.BlockSpec((1, 1, d), per_b),
                  pl.BlockSpec((d, d_ff), const),
                  pl.BlockSpec((d, d_ff), const),
                  pl.BlockSpec((3, 2 * d_ff), const),
                  pl.BlockSpec((1, 2 * d_ff), const),
                  pl.BlockSpec((d_ff, d), const),
                  pl.BlockSpec((1, d), const)],
        out_specs=pl.BlockSpec((1, row_tile, d), lambda b, i: (b, i, 0)),
        out_shape=jax.ShapeDtypeStruct((bsz, t, d), F32),
        scratch_shapes=[pltpu.VMEM((row_tile, d_ff), BF16),
                        pltpu.VMEM((SUBLANES, 2 * d_ff), F32)],
        compiler_params=_params("parallel", "arbitrary"),
        name="conv_glu_ffn",
    )(h, g.reshape(1, d), scale.reshape(bsz, 1, d), shift.reshape(bsz, 1, d), gate.reshape(bsz, 1, d),
      wv, wg, conv_w, conv_b.reshape(1, 2 * d_ff), w_down.astype(BF16), fg)


def _unit_lower_inverse(neg_l):
    c = neg_l.shape[0]
    ri = lax.broadcasted_iota(jnp.int32, (c, c), 0)
    ci = lax.broadcasted_iota(jnp.int32, (c, c), 1)
    p = jnp.where(ri == ci, 1.0, 0.0) + neg_l
    m = neg_l
    power = 1
    while 2 * power < c:
        m = _mm(m, m)
        p = p + _mm(p, m)
        power *= 2
    return p


def _head_rms_gate(o, norm_w, z):
    ms = jnp.mean(o * o, axis=-1, keepdims=True)
    return (o * lax.rsqrt(ms + NORM_EPS) * norm_w) * _silu(z)


def _dn_kernel(q_ref, k_ref, v_ref, z_ref, abt_ref, cwq_ref, cwk_ref, cwv_ref, alog_ref, dtb_ref, nw_ref,
               o_ref, qs_ref, ks_ref, vs_ref, carry_ref, s_ref, s0_ref, *, n_chunks, n_heads, heads_per_step, dk):
    head0 = pl.program_id(1) * heads_per_step

    @pl.when(pl.program_id(2) == 0)
    def _():
        carry_ref[...] = jnp.zeros_like(carry_ref)
        s_ref[...] = jnp.zeros_like(s_ref)

    def conv_silu(x_ref, slot, w_ref):
        x = x_ref[0]
        prev8 = carry_ref[slot]
        w = w_ref[...]
        y = (_shift_rows(x, prev8, 3) * w[0:1] + _shift_rows(x, prev8, 2) * w[1:2]
             + _shift_rows(x, prev8, 1) * w[2:3] + x * w[3:4])
        carry_ref[slot] = x[-SUBLANES:]
        return _silu(y)

    qc = conv_silu(q_ref, 0, cwq_ref)
    kc = conv_silu(k_ref, 1, cwk_ref)
    vs_ref[...] = conv_silu(v_ref, 2, cwv_ref)
    for j in range(heads_per_step):
        hl = slice(j * LANES, (j + 1) * LANES)
        qj, kj = qc[:, hl], kc[:, hl]
        qs_ref[:, hl] = qj * lax.rsqrt(jnp.sum(qj * qj, axis=-1, keepdims=True) + 1e-6) * (dk ** -0.5)
        ks_ref[:, hl] = kj * lax.rsqrt(jnp.sum(kj * kj, axis=-1, keepdims=True) + 1e-6)

    ri = lax.broadcasted_iota(jnp.int32, (CHUNK, CHUNK), 0)
    ci = lax.broadcasted_iota(jnp.int32, (CHUNK, CHUNK), 1)
    nw = nw_ref[...]

    gates = []
    for j in range(heads_per_step):
        a_row = abt_ref[0, pl.ds(head0 + j, 1), :]
        b_row = abt_ref[0, pl.ds(n_heads + head0 + j, 1), :]
        g_all = -jnp.exp(alog_ref[j][:, :1]) * _softplus(a_row + dtb_ref[j][:, :1])
        gates.append((g_all, _sigmoid(b_row)))

    chunks = []
    for c, j in [(c, j) for c in range(n_chunks) for j in range(heads_per_step)]:
        rows = pl.ds(c * CHUNK, CHUNK)
        lanes = slice(c * CHUNK, (c + 1) * CHUNK)
        hl = slice(j * LANES, (j + 1) * LANES)
        g_all, beta_all = gates[j]
        q = qs_ref[rows, hl]
        k = ks_ref[rows, hl]
        v = vs_ref[rows, hl]
        g_row = g_all[:, lanes]
        g_col = jnp.sum(jnp.where(ci <= ri, g_row, 0.0), axis=1, keepdims=True)
        g_rowc = jnp.sum(jnp.where(ci == ri, g_col, 0.0), axis=0, keepdims=True)
        beta_col = jnp.sum(jnp.where(ci == ri, beta_all[:, lanes], 0.0), axis=1, keepdims=True)
        g_last = g_col[CHUNK - 1:CHUNK, :]
        decay = jnp.where(ci <= ri, jnp.exp(jnp.minimum(g_col - g_rowc, 0.0)), 0.0)
        e_g = jnp.exp(g_col)
        kb = k * beta_col
        chunks.append(dict(
            rows=rows, hl=hl, head=j, neg_l=jnp.where(ci < ri, -(_mm_nt(kb, k) * decay), 0.0),
            kbe=(kb * e_g).astype(BF16), vb=(v * beta_col).astype(BF16),
            attn=(_mm_nt(q, k) * decay).astype(BF16), qe=(q * e_g).astype(BF16),
            kd=(k * jnp.exp(g_last - g_col)).astype(BF16), gl=jnp.exp(g_last)))

    eye = jnp.where(ri == ci, 1.0, 0.0)
    ms = [ch["neg_l"] for ch in chunks]
    ps = [eye + m for m in ms]
    power = 1
    while 2 * power < CHUNK:
        ms = [_mm(m, m) for m in ms]
        ps = [p + _mm(p, m) for p, m in zip(ps, ms)]
        power *= 2

    ws = [_mm(p, ch["kbe"]).astype(BF16) for p, ch in zip(ps, chunks)]
    us = [_mm(p, ch["vb"]) for p, ch in zip(ps, chunks)]

    a_s = [(-_mm_tn(w, ch["kd"])).astype(BF16) for w, ch in zip(ws, chunks)]
    n_s = [_mm_tn(u, ch["kd"]) for u, ch in zip(us, chunks)]

    heads = range(heads_per_step)
    states = [s_ref[j] for j in heads]
    for c in range(n_chunks):
        idxs = [c * heads_per_step + j for j in heads]
        sbs = [states[j].astype(BF16) for j in heads]
        for idx, sb in zip(idxs, sbs):
            s0_ref[idx] = sb
        prods = [_mm(sb, a_s[idx]) for idx, sb in zip(idxs, sbs)]
        states = [states[j] * chunks[idx]["gl"] + prods[j] + n_s[idx] for j, idx in zip(heads, idxs)]
    for j in heads:
        s_ref[j] = states[j]

    wq_s = [_mm_nt(jnp.concatenate([w, ch["qe"]], axis=0), s0_ref[idx])
            for idx, (w, ch) in enumerate(zip(ws, chunks))]
    o_s = [wq[CHUNK:] + _mm(ch["attn"], u - wq[:CHUNK]) for wq, u, ch in zip(wq_s, us, chunks)]
    for ch, o in zip(chunks, o_s):
        z = z_ref[0, ch["rows"], ch["hl"]]
        o_ref[0, ch["rows"], ch["hl"]] = _head_rms_gate(o, nw, z).astype(o_ref.dtype)


def _deltanet_core(proj, abt, conv_w, a_log, dt_bias, norm_w, *, time_tile=512, heads_per_step=4):
    bsz, t, _ = proj.shape
    n_heads = a_log.shape[0]
    dv = norm_w.shape[0]
    time_tile = min(time_tile, t)
    assert dv == LANES and proj.shape[2] == 4 * n_heads * LANES and t % time_tile == 0
    assert n_heads % heads_per_step == 0
    hh = n_heads // heads_per_step
    width = heads_per_step * LANES
    blk = lambda off: pl.BlockSpec((1, time_tile, width), lambda b, h, i: (b, i, off + h))
    cblk = lambda off: pl.BlockSpec((conv_w.shape[0], width), lambda b, h, i: (0, off + h))
    per_head = pl.BlockSpec((heads_per_step, 1, LANES), lambda b, h, i: (h, 0, 0))
    bcast = lambda a: jnp.broadcast_to(a.astype(F32)[:, None, None], (n_heads, 1, LANES))
    return pl.pallas_call(
        functools.partial(_dn_kernel, n_chunks=time_tile // CHUNK, n_heads=n_heads,
                          heads_per_step=heads_per_step, dk=LANES),
        grid=(bsz, hh, t // time_tile),
        in_specs=[blk(0), blk(hh), blk(2 * hh), blk(3 * hh),
                  pl.BlockSpec((1, 2 * n_heads, time_tile), lambda b, h, i: (b, 0, i)),
                  cblk(0), cblk(hh), cblk(2 * hh),
                  per_head, per_head,
                  pl.BlockSpec((1, LANES), lambda b, h, i: (0, 0))],
        out_specs=pl.BlockSpec((1, time_tile, width), lambda b, h, i: (b, i, h)),
        out_shape=jax.ShapeDtypeStruct((bsz, t, n_heads * dv), BF16),
        scratch_shapes=[pltpu.VMEM((time_tile, width), F32),
                        pltpu.VMEM((time_tile, width), F32),
                        pltpu.VMEM((time_tile, width), F32),
                        pltpu.VMEM((3, SUBLANES, width), F32),
                        pltpu.VMEM((heads_per_step, LANES, LANES), F32),
                        pltpu.VMEM((time_tile // CHUNK * heads_per_step, LANES, LANES), BF16)],
        compiler_params=_params("parallel", "parallel", "arbitrary"),
        name="deltanet_chunk",
    )(proj, proj, proj, proj, abt, conv_w, conv_w, conv_w, bcast(a_log), bcast(dt_bias), norm_w.reshape(1, dv))


def _deltanet_layer(h, g, scale, shift, gate, w_in, conv_w, a_log, dt_bias, norm_w, w_out):
    n_heads = a_log.shape[0]
    n_main = w_in.shape[1] - 2 * n_heads
    proj, abt = _proj(h, g, scale, shift, w_in[:, :n_main].astype(BF16), w_in[:, n_main:].T)
    og = _deltanet_core(proj, abt, conv_w, a_log, dt_bias, norm_w)
    return _out_proj(og, w_out.astype(BF16), h, gate)


def _gla_levels():
    levels, s = [], CHUNK // 2
    while s >= 1:
        levels.append(s)
        s //= 2
    return levels


def _gla_sum_matrix():
    c = CHUNK
    mats = [np.tril(np.ones((c, c), np.float32))]
    for s in _gla_levels():
        m_l = np.zeros((c, c), np.float32)
        for i in range(c):
            mid = (i // (2 * s)) * 2 * s + s
            if i >= mid:
                m_l[i, mid:i + 1] = 1.0
            else:
                m_l[i, i + 1:mid] = 1.0
        mats.append(m_l)
    return np.concatenate(mats, axis=0)


def _gla_kernel(q_ref, f_ref, i_ref, g_ref, lbl_ref, nw_ref, sm_ref, o_ref, st_ref, *, n_chunks, layer):
    @pl.when(pl.program_id(2) == 0)
    def _():
        st_ref[...] = jnp.zeros_like(st_ref)

    logits = lbl_ref[...]
    e = jnp.exp(logits - jnp.max(logits, axis=0, keepdims=True))
    if layer == 0:
        lb = jnp.zeros((1, LANES), F32)
    else:
        lb = jnp.sum(e[1:layer + 1], axis=0, keepdims=True) / jnp.sum(e, axis=0, keepdims=True)
    nw = nw_ref[...]
    sm = sm_ref[...]
    ri = lax.broadcasted_iota(jnp.int32, (CHUNK, CHUNK), 0)
    ci = lax.broadcasted_iota(jnp.int32, (CHUNK, CHUNK), 1)
    row = lax.broadcasted_iota(jnp.int32, (CHUNK, 1), 0)
    levels = _gla_levels()

    for c in range(n_chunks):
        rows = pl.ds(c * CHUNK, CHUNK)
        q = _silu(q_ref[0, rows, :])
        f = lb + (1.0 - lb) * _sigmoid(f_ref[0, rows, :])
        k = 1.0 - f
        v = i_ref[0, rows, :]
        lf_hi, lf_lo = _split2(jnp.log(f))
        sums = (jnp.dot(sm, lf_hi, preferred_element_type=F32)
                + jnp.dot(sm, lf_lo, preferred_element_type=F32))
        b = sums[:CHUNK]
        b_last = b[CHUNK - 1:CHUNK, :]
        a = jnp.where(ri == ci, _mm_nt(q, k), 0.0)
        for l, s in enumerate(levels):
            ex = jnp.exp(sums[(l + 1) * CHUNK:(l + 2) * CHUNK])
            upper = (row & (2 * s - 1)) >= s
            q_l = jnp.where(upper, q * ex, 0.0)
            k_l = jnp.where(upper, 0.0, k * ex)
            same = (ri // (2 * s)) == (ci // (2 * s))
            a = a + jnp.where(same, _mm_nt(q_l, k_l), 0.0)
        st = st_ref[...]
        o = _mm_nt(q * jnp.exp(b), st) + _mm(a, v)
        st_ref[...] = st * jnp.exp(b_last) + _mm_tn(v, k * jnp.exp(b_last - b))
        o_ref[0, rows, :] = _head_rms_gate(o, nw, g_ref[0, rows, :]).astype(o_ref.dtype)


def _gla_core(proj, lb_logits, layer, norm_w, *, time_tile=512):
    bsz, t, n = proj.shape
    dv = norm_w.shape[0]
    n_heads = n // (4 * LANES)
    time_tile = min(time_tile, t)
    assert dv == LANES and n == 4 * n_heads * LANES and t % time_tile == 0
    blk = lambda off: pl.BlockSpec((1, time_tile, LANES), lambda b, h, i: (b, i, off + h))
    sm = jnp.asarray(_gla_sum_matrix(), BF16)
    return pl.pallas_call(
        functools.partial(_gla_kernel, n_chunks=time_tile // CHUNK, layer=layer),
        grid=(bsz, n_heads, t // time_tile),
        in_specs=[blk(0), blk(n_heads), blk(2 * n_heads), blk(3 * n_heads),
                  pl.BlockSpec((lb_logits.shape[0], LANES), lambda b, h, i: (0, h)),
                  pl.BlockSpec((1, LANES), lambda b, h, i: (0, 0)),
                  pl.BlockSpec(sm.shape, lambda b, h, i: (0, 0))],
        out_specs=pl.BlockSpec((1, time_tile, LANES), lambda b, h, i: (b, i, h)),
        out_shape=jax.ShapeDtypeStruct((bsz, t, n_heads * dv), BF16),
        scratch_shapes=[pltpu.VMEM((LANES, LANES), F32)],
        compiler_params=_params("parallel", "parallel", "arbitrary"),
        name="gla_chunk",
    )(proj, proj, proj, proj, lb_logits.astype(F32), norm_w.reshape(1, dv), sm)


def _hgrn2_layer(h, g, scale, shift, gate, w_in, lb_logits, layer, norm_w, w_out):
    proj = _proj(h, g, scale, shift, w_in.astype(BF16))
    og = _gla_core(proj, lb_logits, layer, norm_w)
    return _out_proj(og, w_out.astype(BF16), h, gate)


def _rwkv_proj_kernel(h_ref, g_ref, sc_ref, sh_ref, mu_ref, wrkv_ref, w0_ref, w1_ref, w2_ref, a0_ref, a1_ref,
                      a2_ref, g1_ref, g2_ref, r_ref, k_ref, v_ref, a_ref, wl_ref, gate_ref, carry_ref):
    @pl.when(pl.program_id(1) == 0)
    def _():
        carry_ref[...] = jnp.zeros_like(carry_ref)

    u = _norm_mod(h_ref[0], g_ref[...], sc_ref[0], sh_ref[0])
    dx = _shift_rows(u, carry_ref[...], 1) - u
    carry_ref[...] = u[-SUBLANES:]
    mu = mu_ref[...]
    mix = lambda s: (u + dx * mu[s:s + 1]).astype(BF16)
    dot = lambda a, b: jnp.dot(a, b, preferred_element_type=F32)
    r_ref[0] = dot(mix(0), wrkv_ref[0])
    k_ref[0] = dot(mix(2), wrkv_ref[1])
    v_ref[0] = dot(mix(3), wrkv_ref[2])
    w = w0_ref[...] + dot(jnp.tanh(dot(mix(1), w1_ref[...])).astype(BF16), w2_ref[...])
    w = -_softplus(-w) - 0.5
    wl_ref[0] = -jnp.exp(w)
    a_ref[0] = _sigmoid(a0_ref[...] + dot(dot(mix(4), a1_ref[...]).astype(BF16), a2_ref[...]))
    gate_ref[0] = dot(_sigmoid(dot(mix(5), g1_ref[...])).astype(BF16), g2_ref[...]).astype(gate_ref.dtype)


def _pad_lora(w_in, w_out):
    rank = w_in.shape[1]
    pad = -(-rank // LANES) * LANES - rank
    return (jnp.pad(w_in, ((0, 0), (0, pad))).astype(BF16), jnp.pad(w_out, ((0, pad), (0, 0))).astype(BF16))


def _rwkv_proj(h, g, scale, shift, mu, w_rkv, w0, w1, w2, a0, a1, a2, g1, g2, *, row_tile=256):
    bsz, t, d = h.shape
    row_tile = min(row_tile, t)
    assert t % row_tile == 0
    w1p, w2p = _pad_lora(w1, w2)
    a1p, a2p = _pad_lora(a1, a2)
    g1p, g2p = _pad_lora(g1, g2)
    const2 = lambda b, i: (0, 0)
    per_b = lambda b, i: (b, 0, 0)
    tile = pl.BlockSpec((1, row_tile, d), lambda b, i: (b, i, 0))
    full = lambda a: pl.BlockSpec(a.shape, const2)
    vec = pl.BlockSpec((1, d), const2)
    f32_out = jax.ShapeDtypeStruct((bsz, t, d), F32)
    return pl.pallas_call(
        _rwkv_proj_kernel,
        grid=(bsz, t // row_tile),
        in_specs=[tile, vec, pl.BlockSpec((1, 1, d), per_b), pl.BlockSpec((1, 1, d), per_b),
                  full(mu), pl.BlockSpec(w_rkv.shape, lambda b, i: (0, 0, 0)),
                  vec, full(w1p), full(w2p), vec, full(a1p), full(a2p), full(g1p), full(g2p)],
        out_specs=[tile] * 6,
        out_shape=[f32_out] * 5 + [jax.ShapeDtypeStruct((bsz, t, d), BF16)],
        scratch_shapes=[pltpu.VMEM((SUBLANES, d), F32)],
        compiler_params=_params("parallel", "arbitrary"),
        name="rwkv_proj",
    )(h, g.reshape(1, d), scale.reshape(bsz, 1, d), shift.reshape(bsz, 1, d), mu, w_rkv.astype(BF16),
      w0.reshape(1, d), w1p, w2p, a0.reshape(1, d), a1p, a2p, g1p, g2p)


def _rwkv_chunk_kernel(r_ref, k_ref, v_ref, a_ref, wl_ref, gate_ref, kk_ref, ka_ref, rk_ref, lnw_ref, lnb_ref,
                       tri_ref, o_ref, s_ref, at_ref, rt_ref, bt_ref, kt_ref, bonus_ref, y_ref, s0_ref,
                       *, n_chunks, pairs_per_step, head_size, gn_eps):
    @pl.when(pl.program_id(2) == 0)
    def _():
        s_ref[...] = jnp.zeros_like(s_ref)

    c_len = CHUNK
    lane = lax.broadcasted_iota(jnp.int32, (1, LANES), 1)
    lo = lane < head_size
    lane_t = lax.broadcasted_iota(jnp.int32, (c_len, LANES), 1) & (head_size - 1)
    row_t = lax.broadcasted_iota(jnp.int32, (c_len, LANES), 0)
    strict = lane_t < row_t
    incl = lane_t <= row_t
    eye2 = jnp.where(lane_t == row_t, 1.0, 0.0)
    bd_r = lax.broadcasted_iota(jnp.int32, (LANES, LANES), 0) < head_size
    bd_c = lax.broadcasted_iota(jnp.int32, (LANES, LANES), 1) < head_size
    block_diag = bd_r == bd_c
    tri = tri_ref[...]

    def pair_sum(x):
        s_lo = jnp.sum(jnp.where(lo, x, 0.0), axis=-1, keepdims=True)
        s_hi = jnp.sum(jnp.where(lo, 0.0, x), axis=-1, keepdims=True)
        return jnp.where(lo, s_lo, s_hi)

    def stack(x, first_lo=True):
        x_lo, x_hi = jnp.where(lo, x, 0.0), jnp.where(lo, 0.0, x)
        return jnp.concatenate([x_lo, x_hi] if first_lo else [x_hi, x_lo], axis=0)

    pairs = range(pairs_per_step)
    lanes_of = lambda p: slice(p * LANES, (p + 1) * LANES)
    kk_all, k_all = [], []
    for p in pairs:
        pl_ = lanes_of(p)
        k_raw = k_ref[0, :, pl_]
        a = a_ref[0, :, pl_]
        kx = k_raw * kk_ref[:, pl_]
        kk_all.append(kx * lax.rsqrt(pair_sum(kx * kx) + 1e-6))
        k_all.append(k_raw * (1.0 + (a - 1.0) * ka_ref[:, pl_]))
        bonus_ref[:, pl_] = pair_sum(r_ref[0, :, pl_] * k_all[p] * rk_ref[:, pl_]) * v_ref[0, :, pl_]

    cums = []
    for c in range(n_chunks):
        wl_hi, wl_lo = _split2(wl_ref[0, pl.ds(c * c_len, c_len), :])
        cums.append(jnp.dot(tri, wl_hi, preferred_element_type=F32)
                    + jnp.dot(tri, wl_lo, preferred_element_type=F32))
    cum_all = jnp.concatenate(cums, axis=0)
    gam_last = [jnp.exp(cm[c_len - 1:c_len, :]) for cm in cums]

    for p in pairs:
        pl_ = lanes_of(p)
        cum = cum_all[:, pl_]
        inv_gam = jnp.exp(-cum)
        at_ref[:, pl_] = (-kk_all[p] * jnp.exp(cum - wl_ref[0, :, pl_])).astype(BF16)
        rt_ref[:, pl_] = (r_ref[0, :, pl_] * jnp.exp(cum)).astype(BF16)
        bt_ref[:, pl_] = (kk_all[p] * a_ref[0, :, pl_] * inv_gam).astype(BF16)
        kt_ref[:, pl_] = (k_all[p] * inv_gam).astype(BF16)

    items = []
    for c, p in [(c, p) for c in range(n_chunks) for p in pairs]:
        rows = pl.ds(c * c_len, c_len)
        pl_ = lanes_of(p)
        a_t = at_ref[rows, pl_]
        r_t = rt_ref[rows, pl_]
        b_t = bt_ref[rows, pl_]
        k_t = kt_ref[rows, pl_]
        v = v_ref[0, rows, pl_]
        bk = jnp.concatenate([b_t, k_t], axis=0)
        ar = jnp.concatenate([a_t, r_t], axis=0)
        zero = jnp.zeros_like(ar)
        r1 = _mm_nt(jnp.where(lo, ar, zero), bk)
        r2 = _mm_nt(jnp.where(lo, zero, ar), jnp.concatenate([k_t, b_t], axis=0))
        a_ab = jnp.where(strict, jnp.where(lo, r1[:c_len], r2[:c_len]), 0.0)
        a_ak = jnp.where(strict, jnp.where(lo, r2[:c_len], r1[:c_len]), 0.0)
        a_rb = jnp.where(incl, jnp.where(lo, r1[c_len:], r2[c_len:]), 0.0)
        a_rk = jnp.where(incl, jnp.where(lo, r2[c_len:], r1[c_len:]), 0.0)
        items.append(dict(
            rows=rows, pl=pl_, pair=p, a_ab=a_ab, a_ak=a_ak.astype(BF16),
            a_r=jnp.concatenate([a_rb, a_rk], axis=1).astype(BF16),
            a_t=a_t, r_t=r_t, bk=bk, v_sw=stack(v, first_lo=False).astype(BF16), v=v,
            gl_row=gam_last[c][:, pl_]))

    ms = [it["a_ab"] for it in items]
    ts = [eye2 + m for m in ms]
    power = 1
    while 2 * power < c_len:
        ms = [_mm(m, stack(m)) for m in ms]
        ts = [t + _mm(t, stack(m)) for t, m in zip(ts, ms)]
        power *= 2

    w_ts = [_mm(t, stack(it["a_t"])).astype(BF16) for t, it in zip(ts, items)]
    akvs = [_mm(it["a_ak"], it["v_sw"]) for it in items]
    u_0s = [_mm(t, stack(akv)) for t, akv in zip(ts, akvs)]

    m_s = [(jnp.where(block_diag, _mm_tn(w_t, it["bk"][:c_len]), 0.0) * it["gl_row"]).astype(BF16)
           for w_t, it in zip(w_ts, items)]
    n_s = [jnp.where(block_diag, _mm_tn(jnp.concatenate([u_0, it["v"]], axis=0), it["bk"]), 0.0) * it["gl_row"]
           for u_0, it in zip(u_0s, items)]

    states = [s_ref[p] for p in pairs]
    for c in range(n_chunks):
        idxs = [c * pairs_per_step + p for p in pairs]
        sbs = [states[p].astype(BF16) for p in pairs]
        for idx, sb in zip(idxs, sbs):
            s0_ref[idx] = sb
        prods = [_mm(sb, m_s[idx]) for idx, sb in zip(idxs, sbs)]
        states = [states[p] * items[idx]["gl_row"] + prods[p] + n_s[idx] for p, idx in zip(pairs, idxs)]
    for p in pairs:
        s_ref[p] = states[p]

    wr_s = [_mm_nt(jnp.concatenate([w_t, it["r_t"]], axis=0), s0_ref[idx])
            for idx, (it, w_t) in enumerate(zip(items, w_ts))]
    y_s = [wr[c_len:] + _mm(it["a_r"], jnp.concatenate([stack(wr[:c_len] + u_0).astype(BF16), it["v_sw"]], axis=0))
           for it, wr, u_0 in zip(items, wr_s, u_0s)]
    for it, y in zip(items, y_s):
        y_ref[it["rows"], it["pl"]] = y

    inv_n = 1.0 / head_size
    for p in pairs:
        pl_ = lanes_of(p)
        y = y_ref[:, pl_]
        mean = pair_sum(y) * inv_n
        yc = y - mean
        var = pair_sum(yc * yc) * inv_n
        y_n = yc * lax.rsqrt(var + gn_eps) * lnw_ref[:, pl_] + lnb_ref[:, pl_]
        o_ref[0, :, pl_] = ((y_n + bonus_ref[:, pl_]) * gate_ref[0, :, pl_].astype(F32)).astype(o_ref.dtype)


def _rwkv_chunk(r, k, v, a, wl, gate, k_k, k_a, r_k, ln_w, ln_b, *, time_tile=512, pairs_per_step=4):
    bsz, t, d = r.shape
    head_size = r_k.shape[1]
    assert 2 * head_size == LANES and d % (LANES * pairs_per_step) == 0
    time_tile = min(time_tile, t)
    assert t % time_tile == 0
    width = LANES * pairs_per_step
    n_pairs = d // width
    blk = pl.BlockSpec((1, time_tile, width), lambda b, p, i: (b, i, p))
    vec = pl.BlockSpec((1, width), lambda b, p, i: (0, p))
    tri = jnp.asarray(np.tril(np.ones((CHUNK, CHUNK), np.float32)), BF16)
    flat = lambda x: x.reshape(1, d).astype(F32)
    return pl.pallas_call(
        functools.partial(_rwkv_chunk_kernel, n_chunks=time_tile // CHUNK, pairs_per_step=pairs_per_step,
                          head_size=head_size, gn_eps=1e-5 * head_size * head_size),
        grid=(bsz, n_pairs, t // time_tile),
        in_specs=[blk] * 6 + [vec] * 5 + [pl.BlockSpec((CHUNK, CHUNK), lambda b, p, i: (0, 0))],
        out_specs=blk,
        out_shape=jax.ShapeDtypeStruct((bsz, t, d), BF16),
        scratch_shapes=[pltpu.VMEM((pairs_per_step, LANES, LANES), F32)]
        + [pltpu.VMEM((time_tile, width), BF16)] * 4 + [pltpu.VMEM((time_tile, width), F32)] * 2
        + [pltpu.VMEM((time_tile // CHUNK * pairs_per_step, LANES, LANES), BF16)],
        compiler_params=_params("parallel", "parallel", "arbitrary"),
        name="rwkv_chunk",
    )(r, k, v, a, wl, gate, flat(k_k), flat(k_a), flat(r_k), flat(ln_w), flat(ln_b), tri)


def _rwkv7_layer(h, g, scale, shift, gate, mu, w_rkv, w0, w1, w2, a0, a1, a2, g1, g2, k_k, k_a, r_k, ln_w, ln_b,
                 w_out):
    r, k, v, a, wl, gt = _rwkv_proj(h, g, scale, shift, mu, w_rkv, w0, w1, w2, a0, a1, a2, g1, g2)
    yg = _rwkv_chunk(r, k, v, a, wl, gt, k_k, k_a, r_k, ln_w, ln_b)
    return _out_proj(yg, w_out.astype(BF16), h, gate)


def kernel(x, c, norm_g, ada_w, ada_b, a_w_in, a_conv, a_A_log, a_dt_bias, a_norm, a_w_out, b_w_in, hgrn_lb_logits, b_norm, b_w_out, c_mu, c_w_rkv, c_w0, c_w1, c_w2, c_a0, c_a1, c_a2, c_g1, c_g2, c_k_k, c_k_a, c_r_k, c_ln_w, c_ln_b, c_w_out, f_w_up, f_conv_w, f_conv_b, f_w_down, final_g):
    depth = ada_w.shape[0]
    n_mixers = 3
    mod = _ada_mod(c, ada_w, ada_b)
    h = x
    for i in range(depth):
        shift1, scale1, gate1, shift2, scale2, gate2 = (mod[i, :, s] for s in range(6))
        kind, j = i % n_mixers, i // n_mixers
        if kind == 0:
            h = _deltanet_layer(h, norm_g[i, 0], scale1, shift1, gate1, a_w_in[j], a_conv[j], a_A_log[j],
                                a_dt_bias[j], a_norm[j], a_w_out[j])
        elif kind == 1:
            h = _hgrn2_layer(h, norm_g[i, 0], scale1, shift1, gate1, b_w_in[j], hgrn_lb_logits, i, b_norm[j],
                             b_w_out[j])
        else:
            h = _rwkv7_layer(h, norm_g[i, 0], scale1, shift1, gate1, c_mu[j], c_w_rkv[j], c_w0[j], c_w1[j],
                             c_w2[j], c_a0[j], c_a1[j], c_a2[j], c_g1[j], c_g2[j], c_k_k[j], c_k_a[j], c_r_k[j],
                             c_ln_w[j], c_ln_b[j], c_w_out[j])
        h = _ffn(h, norm_g[i, 1], scale2, shift2, gate2, f_w_up[i], f_conv_w[i], f_conv_b[i], f_w_down[i],
                 final_g if i == depth - 1 else None)
    return h
```

```python
import functools
import math

import numpy as np
import jax
import jax.numpy as jnp
from jax import lax
from jax.experimental import pallas as pl
from jax.experimental.pallas import tpu as pltpu

F32 = jnp.float32
BF16 = jnp.bfloat16

CHUNK = 64
NORM_EPS = 1e-6
LANES = 128
SUBLANES = 8
VMEM_LIMIT_BYTES = 56 * 1024 * 1024


def _params(*sem):
    return pltpu.CompilerParams(dimension_semantics=sem, vmem_limit_bytes=VMEM_LIMIT_BYTES)


def _mm(a, b):
    return jnp.dot(a.astype(BF16), b.astype(BF16), preferred_element_type=F32)


def _mm_nt(a, b):
    return lax.dot_general(a.astype(BF16), b.astype(BF16), (((1,), (1,)), ((), ())),
                           preferred_element_type=F32)


def _mm_tn(a, b):
    return lax.dot_general(a.astype(BF16), b.astype(BF16), (((0,), (0,)), ((), ())),
                           preferred_element_type=F32)


def _split2(x):
    hi = x.astype(BF16)
    lo = (x - hi.astype(F32)).astype(BF16)
    return hi, lo


def _sigmoid(x):
    return 1.0 / (1.0 + jnp.exp(-x))


def _silu(x):
    return x * _sigmoid(x)


def _softplus(x):
    return jnp.maximum(x, 0.0) + jnp.log(1.0 + jnp.exp(-jnp.abs(x)))


def _norm_mod(h, g, scale, shift):
    ms = jnp.mean(h * h, axis=-1, keepdims=True)
    y = h * lax.rsqrt(ms + NORM_EPS)
    return (y * g) * (1.0 + scale) + shift


def _shift_rows(x, prev8, s):
    rolled = pltpu.roll(x, s, axis=0)
    head = pltpu.roll(prev8, s, axis=0)
    row = lax.broadcasted_iota(jnp.int32, (SUBLANES, x.shape[1]), 0)
    top = jnp.where(row < s, head, rolled[:SUBLANES])
    return jnp.concatenate([top, rolled[SUBLANES:]], axis=0)


def _mod_kernel(c_ref, w_ref, b_ref, o_ref):
    c = c_ref[...]
    o_ref[0] = _mm(_silu(c), w_ref[0]) + b_ref[0]


def _ada_mod(c, ada_w, ada_b):
    depth, d, d6 = ada_w.shape
    bsz = c.shape[0]
    rows = -(-bsz // SUBLANES) * SUBLANES
    c_pad = jnp.zeros((rows, d), F32).at[:bsz].set(c)
    out = pl.pallas_call(
        _mod_kernel,
        grid=(depth, d6 // d),
        in_specs=[pl.BlockSpec((rows, d), lambda i, j: (0, 0)),
                  pl.BlockSpec((1, d, d), lambda i, j: (i, 0, j)),
                  pl.BlockSpec((1, 1, d), lambda i, j: (i, 0, j))],
        out_specs=pl.BlockSpec((1, rows, d), lambda i, j: (i, 0, j)),
        out_shape=jax.ShapeDtypeStruct((depth, rows, d6), F32),
        compiler_params=_params("parallel", "parallel"),
        name="ada_mod",
    )(c_pad, ada_w, ada_b.reshape(depth, 1, d6))
    return out[:, :bsz].reshape(depth, bsz, d6 // d, d)


def _proj_kernel(h_ref, g_ref, sc_ref, sh_ref, w_ref, *rest, n_col_tiles, col_tile, has_small):
    if has_small:
        wst_ref, o_ref, ost_ref = rest
    else:
        (o_ref,) = rest
    u = _norm_mod(h_ref[0], g_ref[...], sc_ref[0], sh_ref[0])
    ub = u.astype(BF16)
    for j in range(n_col_tiles):
        sl = slice(j * col_tile, (j + 1) * col_tile)
        o_ref[0, :, sl] = jnp.dot(ub, w_ref[:, sl], preferred_element_type=F32).astype(o_ref.dtype)
    if has_small:
        u_hi, u_lo = _split2(u)
        w_hi, w_lo = _split2(wst_ref[...])
        ost_ref[0] = _mm_nt(w_hi, u_hi) + _mm_nt(w_hi, u_lo) + _mm_nt(w_lo, u_hi)


def _proj(h, g, scale, shift, w, w_small_t=None, *, row_tile=512, col_tile=1024, out_dtype=F32):
    bsz, t, d = h.shape
    n = w.shape[1]
    col_tile = min(col_tile, n)
    row_tile = min(row_tile, t)
    assert t % row_tile == 0 and n % col_tile == 0
    has_small = w_small_t is not None
    in_specs = [pl.BlockSpec((1, row_tile, d), lambda b, i: (b, i, 0)),
                pl.BlockSpec((1, d), lambda b, i: (0, 0)),
                pl.BlockSpec((1, 1, d), lambda b, i: (b, 0, 0)),
                pl.BlockSpec((1, 1, d), lambda b, i: (b, 0, 0)),
                pl.BlockSpec((d, n), lambda b, i: (0, 0))]
    args = [h, g.reshape(1, d), scale.reshape(bsz, 1, d), shift.reshape(bsz, 1, d), w]
    out_specs = [pl.BlockSpec((1, row_tile, n), lambda b, i: (b, i, 0))]
    out_shape = [jax.ShapeDtypeStruct((bsz, t, n), out_dtype)]
    if has_small:
        r = w_small_t.shape[0]
        in_specs.append(pl.BlockSpec((r, d), lambda b, i: (0, 0)))
        args.append(w_small_t)
        out_specs.append(pl.BlockSpec((1, r, row_tile), lambda b, i: (b, 0, i)))
        out_shape.append(jax.ShapeDtypeStruct((bsz, r, t), F32))
    outs = pl.pallas_call(
        functools.partial(_proj_kernel, n_col_tiles=n // col_tile, col_tile=col_tile, has_small=has_small),
        grid=(bsz, t // row_tile),
        in_specs=in_specs, out_specs=out_specs, out_shape=out_shape,
        compiler_params=_params("parallel", "parallel"),
        name="norm_proj",
    )(*args)
    return outs if has_small else outs[0]


def _ffn_kernel(x_ref, wo_ref, gate1_ref, h_ref, g_ref, sc_ref, sh_ref, gate_ref, wv_ref, wg_ref, cw_ref, cb_ref,
                wd_ref, fg_ref, o_ref, act_ref, carry_ref, *, d_ff, col_tile, final_norm):
    i = pl.program_id(1)
    h = h_ref[0] + gate1_ref[0] * jnp.dot(x_ref[0], wo_ref[...], preferred_element_type=F32)
    ub = _norm_mod(h, g_ref[...], sc_ref[0], sh_ref[0]).astype(BF16)

    @pl.when(i == 0)
    def _():
        carry_ref[...] = jnp.zeros_like(carry_ref)

    def conv(up, prev8, w3, bias):
        return (_shift_rows(up, prev8, 2) * w3[0:1] + _shift_rows(up, prev8, 1) * w3[1:2]
                + up * w3[2:3] + bias)

    for j in range(d_ff // col_tile):
        sl = slice(j * col_tile, (j + 1) * col_tile)
        sl_g = slice(d_ff + j * col_tile, d_ff + (j + 1) * col_tile)
        up_v = jnp.dot(ub, wv_ref[:, sl], preferred_element_type=F32)
        up_g = jnp.dot(ub, wg_ref[:, sl], preferred_element_type=F32)
        val = conv(up_v, carry_ref[:, sl], cw_ref[:, sl], cb_ref[:, sl])
        gat = conv(up_g, carry_ref[:, sl_g], cw_ref[:, sl_g], cb_ref[:, sl_g])
        carry_ref[:, sl] = up_v[-SUBLANES:]
        carry_ref[:, sl_g] = up_g[-SUBLANES:]
        act_ref[:, sl] = (val * _silu(gat)).astype(BF16)

    y = jnp.dot(act_ref[...], wd_ref[...], preferred_element_type=F32)
    out = h + gate_ref[0] * y
    if final_norm:
        ms = jnp.mean(out * out, axis=-1, keepdims=True)
        out = out * lax.rsqrt(ms + NORM_EPS) * fg_ref[...]
    o_ref[0] = out


def _mix_out_ffn(x, w_out, gate1, h, g, scale, shift, gate, w_up, conv_w, conv_b, w_down, final_g=None, *,
                 row_tile=512, col_tile=256):
    bsz, t, d = h.shape
    k_in = x.shape[2]
    d_ff = w_down.shape[0]
    row_tile = min(row_tile, t)
    assert t % row_tile == 0 and d_ff % col_tile == 0
    final_norm = final_g is not None
    fg = (final_g if final_norm else jnp.ones((d,), F32)).reshape(1, d)
    wv = w_up[:, :d_ff].astype(BF16)
    wg = w_up[:, d_ff:].astype(BF16)
    const = lambda b, i: (0, 0)
    per_b = lambda b, i: (b, 0, 0)
    tile = lambda width: pl.BlockSpec((1, row_tile, width), lambda b, i: (b, i, 0))
    resident = lambda shape: pl.BlockSpec(shape, const, pipeline_mode=pl.Buffered(1))
    vec = lambda: pl.BlockSpec((1, 1, d), per_b)
    return pl.pallas_call(
        functools.partial(_ffn_kernel, d_ff=d_ff, col_tile=col_tile, final_norm=final_norm),
        grid=(bsz, t // row_tile),
        in_specs=[tile(k_in), resident((k_in, d)), vec(),
                  tile(d), pl.BlockSpec((1, d), const), vec(), vec(), vec(),
                  resident((d, d_ff)), resident((d, d_ff)),
                  pl.BlockSpec((3, 2 * d_ff), const),
                  pl.BlockSpec((1, 2 * d_ff), const),
                  resident((d_ff, d)),
                  pl.BlockSpec((1, d), const)],
        out_specs=tile(d),
        out_shape=jax.ShapeDtypeStruct((bsz, t, d), F32),
        scratch_shapes=[pltpu.VMEM((row_tile, d_ff), BF16),
                        pltpu.VMEM((SUBLANES, 2 * d_ff), F32)],
        compiler_params=_params("parallel", "arbitrary"),
        name="mix_out_conv_glu_ffn",
    )(x, w_out.astype(BF16), gate1.reshape(bsz, 1, d),
      h, g.reshape(1, d), scale.reshape(bsz, 1, d), shift.reshape(bsz, 1, d), gate.reshape(bsz, 1, d),
      wv, wg, conv_w, conv_b.reshape(1, 2 * d_ff), w_down.astype(BF16), fg)


def _unit_lower_inverse(neg_l):
    c = neg_l.shape[0]
    ri = lax.broadcasted_iota(jnp.int32, (c, c), 0)
    ci = lax.broadcasted_iota(jnp.int32, (c, c), 1)
    p = jnp.where(ri == ci, 1.0, 0.0) + neg_l
    m = neg_l
    power = 1
    while 2 * power < c:
        m = _mm(m, m)
        p = p + _mm(p, m)
        power *= 2
    return p


def _head_rms_gate(o, norm_w, z):
    ms = jnp.mean(o * o, axis=-1, keepdims=True)
    return (o * lax.rsqrt(ms + NORM_EPS) * norm_w) * _silu(z)


def _dn_kernel(q_ref, k_ref, v_ref, z_ref, abt_ref, cwq_ref, cwk_ref, cwv_ref, alog_ref, dtb_ref, nw_ref,
               o_ref, qs_ref, ks_ref, vs_ref, carry_ref, s_ref, s0_ref, *, n_chunks, n_heads, heads_per_step, dk):
    head0 = pl.program_id(1) * heads_per_step

    @pl.when(pl.program_id(2) == 0)
    def _():
        carry_ref[...] = jnp.zeros_like(carry_ref)
        s_ref[...] = jnp.zeros_like(s_ref)

    def conv_silu(x_ref, slot, w_ref):
        x = x_ref[0].astype(F32)
        prev8 = carry_ref[slot]
        w = w_ref[...]
        y = (_shift_rows(x, prev8, 3) * w[0:1] + _shift_rows(x, prev8, 2) * w[1:2]
             + _shift_rows(x, prev8, 1) * w[2:3] + x * w[3:4])
        carry_ref[slot] = x[-SUBLANES:]
        return _silu(y)

    qc = conv_silu(q_ref, 0, cwq_ref)
    kc = conv_silu(k_ref, 1, cwk_ref)
    vs_ref[...] = conv_silu(v_ref, 2, cwv_ref)
    for j in range(heads_per_step):
        hl = slice(j * LANES, (j + 1) * LANES)
        qj, kj = qc[:, hl], kc[:, hl]
        qs_ref[:, hl] = qj * lax.rsqrt(jnp.sum(qj * qj, axis=-1, keepdims=True) + 1e-6) * (dk ** -0.5)
        ks_ref[:, hl] = kj * lax.rsqrt(jnp.sum(kj * kj, axis=-1, keepdims=True) + 1e-6)

    ri = lax.broadcasted_iota(jnp.int32, (CHUNK, CHUNK), 0)
    ci = lax.broadcasted_iota(jnp.int32, (CHUNK, CHUNK), 1)
    nw = nw_ref[...]

    gates = []
    for j in range(heads_per_step):
        a_row = abt_ref[0, pl.ds(head0 + j, 1), :]
        b_row = abt_ref[0, pl.ds(n_heads + head0 + j, 1), :]
        g_all = -jnp.exp(alog_ref[j][:, :1]) * _softplus(a_row + dtb_ref[j][:, :1])
        gates.append((g_all, _sigmoid(b_row)))

    chunks = []
    for c, j in [(c, j) for c in range(n_chunks) for j in range(heads_per_step)]:
        rows = pl.ds(c * CHUNK, CHUNK)
        lanes = slice(c * CHUNK, (c + 1) * CHUNK)
        hl = slice(j * LANES, (j + 1) * LANES)
        g_all, beta_all = gates[j]
        q = qs_ref[rows, hl]
        k = ks_ref[rows, hl]
        v = vs_ref[rows, hl]
        g_row = g_all[:, lanes]
        g_col = jnp.sum(jnp.where(ci <= ri, g_row, 0.0), axis=1, keepdims=True)
        g_rowc = jnp.sum(jnp.where(ci == ri, g_col, 0.0), axis=0, keepdims=True)
        beta_col = jnp.sum(jnp.where(ci == ri, beta_all[:, lanes], 0.0), axis=1, keepdims=True)
        g_last = g_col[CHUNK - 1:CHUNK, :]
        decay = jnp.where(ci <= ri, jnp.exp(jnp.minimum(g_col - g_rowc, 0.0)), 0.0)
        e_g = jnp.exp(g_col)
        kb = k * beta_col
        chunks.append(dict(
            rows=rows, hl=hl, head=j, neg_l=jnp.where(ci < ri, -(_mm_nt(kb, k) * decay), 0.0),
            kbe=(kb * e_g).astype(BF16), vb=(v * beta_col).astype(BF16),
            attn=(_mm_nt(q, k) * decay).astype(BF16), qe=(q * e_g).astype(BF16),
            kd=(k * jnp.exp(g_last - g_col)).astype(BF16), gl=jnp.exp(g_last)))

    eye = jnp.where(ri == ci, 1.0, 0.0)
    ms = [ch["neg_l"] for ch in chunks]
    ps = [eye + m for m in ms]
    power = 1
    while 2 * power < CHUNK:
        ms = [_mm(m, m) for m in ms]
        ps = [p + _mm(p, m) for p, m in zip(ps, ms)]
        power *= 2

    ws = [_mm(p, ch["kbe"]).astype(BF16) for p, ch in zip(ps, chunks)]
    us = [_mm(p, ch["vb"]) for p, ch in zip(ps, chunks)]

    a_s = [(-_mm_tn(w, ch["kd"])).astype(BF16) for w, ch in zip(ws, chunks)]
    n_s = [_mm_tn(u, ch["kd"]) for u, ch in zip(us, chunks)]

    heads = range(heads_per_step)
    states = [s_ref[j] for j in heads]
    for c in range(n_chunks):
        idxs = [c * heads_per_step + j for j in heads]
        sbs = [states[j].astype(BF16) for j in heads]
        for idx, sb in zip(idxs, sbs):
            s0_ref[idx] = sb
        prods = [_mm(sb, a_s[idx]) for idx, sb in zip(idxs, sbs)]
        states = [states[j] * chunks[idx]["gl"] + prods[j] + n_s[idx] for j, idx in zip(heads, idxs)]
    for j in heads:
        s_ref[j] = states[j]

    wq_s = [_mm_nt(jnp.concatenate([w, ch["qe"]], axis=0), s0_ref[idx])
            for idx, (w, ch) in enumerate(zip(ws, chunks))]
    o_s = [wq[CHUNK:] + _mm(ch["attn"], u - wq[:CHUNK]) for wq, u, ch in zip(wq_s, us, chunks)]
    for ch, o in zip(chunks, o_s):
        z = z_ref[0, ch["rows"], ch["hl"]].astype(F32)
        o_ref[0, ch["rows"], ch["hl"]] = _head_rms_gate(o, nw, z).astype(o_ref.dtype)


def _deltanet_core(proj, abt, conv_w, a_log, dt_bias, norm_w, *, time_tile=512, heads_per_step=4):
    bsz, t, _ = proj.shape
    n_heads = a_log.shape[0]
    dv = norm_w.shape[0]
    time_tile = min(time_tile, t)
    assert dv == LANES and proj.shape[2] == 4 * n_heads * LANES and t % time_tile == 0
    assert n_heads % heads_per_step == 0
    hh = n_heads // heads_per_step
    width = heads_per_step * LANES
    blk = lambda off: pl.BlockSpec((1, time_tile, width), lambda b, h, i: (b, i, off + h))
    cblk = lambda off: pl.BlockSpec((conv_w.shape[0], width), lambda b, h, i: (0, off + h))
    per_head = pl.BlockSpec((heads_per_step, 1, LANES), lambda b, h, i: (h, 0, 0))
    bcast = lambda a: jnp.broadcast_to(a.astype(F32)[:, None, None], (n_heads, 1, LANES))
    return pl.pallas_call(
        functools.partial(_dn_kernel, n_chunks=time_tile // CHUNK, n_heads=n_heads,
                          heads_per_step=heads_per_step, dk=LANES),
        grid=(bsz, hh, t // time_tile),
        in_specs=[blk(0), blk(hh), blk(2 * hh), blk(3 * hh),
                  pl.BlockSpec((1, 2 * n_heads, time_tile), lambda b, h, i: (b, 0, i)),
                  cblk(0), cblk(hh), cblk(2 * hh),
                  per_head, per_head,
                  pl.BlockSpec((1, LANES), lambda b, h, i: (0, 0))],
        out_specs=pl.BlockSpec((1, time_tile, width), lambda b, h, i: (b, i, h)),
        out_shape=jax.ShapeDtypeStruct((bsz, t, n_heads * dv), BF16),
        scratch_shapes=[pltpu.VMEM((time_tile, width), F32),
                        pltpu.VMEM((time_tile, width), F32),
                        pltpu.VMEM((time_tile, width), F32),
                        pltpu.VMEM((3, SUBLANES, width), F32),
                        pltpu.VMEM((heads_per_step, LANES, LANES), F32),
                        pltpu.VMEM((time_tile // CHUNK * heads_per_step, LANES, LANES), BF16)],
        compiler_params=_params("parallel", "parallel", "arbitrary"),
        name="deltanet_chunk",
    )(proj, proj, proj, proj, abt, conv_w, conv_w, conv_w, bcast(a_log), bcast(dt_bias), norm_w.reshape(1, dv))


def _deltanet_mixer(h, g, scale, shift, w_in, conv_w, a_log, dt_bias, norm_w):
    n_heads = a_log.shape[0]
    n_main = w_in.shape[1] - 2 * n_heads
    proj, abt = _proj(h, g, scale, shift, w_in[:, :n_main].astype(BF16), w_in[:, n_main:].T, out_dtype=BF16)
    return _deltanet_core(proj, abt, conv_w, a_log, dt_bias, norm_w)


def _gla_levels():
    levels, s = [], CHUNK // 2
    while s >= 1:
        levels.append(s)
        s //= 2
    return levels


def _gla_sum_matrix():
    c = CHUNK
    mats = [np.tril(np.ones((c, c), np.float32))]
    for s in _gla_levels():
        m_l = np.zeros((c, c), np.float32)
        for i in range(c):
            mid = (i // (2 * s)) * 2 * s + s
            if i >= mid:
                m_l[i, mid:i + 1] = 1.0
            else:
                m_l[i, i + 1:mid] = 1.0
        mats.append(m_l)
    return np.concatenate(mats, axis=0)


def _gla_kernel(q_ref, f_ref, i_ref, g_ref, lbl_ref, nw_ref, sm_ref, o_ref, st_ref, *, n_chunks, heads_per_step,
                layer):
    @pl.when(pl.program_id(2) == 0)
    def _():
        st_ref[...] = jnp.zeros_like(st_ref)

    logits = lbl_ref[...]
    e = jnp.exp(logits - jnp.max(logits, axis=0, keepdims=True))
    if layer == 0:
        lb = jnp.zeros((1, logits.shape[1]), F32)
    else:
        lb = jnp.sum(e[1:layer + 1], axis=0, keepdims=True) / jnp.sum(e, axis=0, keepdims=True)
    nw = nw_ref[...]
    sm = sm_ref[...]
    ri = lax.broadcasted_iota(jnp.int32, (CHUNK, CHUNK), 0)
    ci = lax.broadcasted_iota(jnp.int32, (CHUNK, CHUNK), 1)
    row = lax.broadcasted_iota(jnp.int32, (CHUNK, 1), 0)
    levels = _gla_levels()
    heads = range(heads_per_step)
    chunk_rows = lambda c: slice(c * CHUNK, (c + 1) * CHUNK)
    head_lanes = lambda j: slice(j * LANES, (j + 1) * LANES)

    q_all = _silu(q_ref[0].astype(F32))
    f_all = lb + (1.0 - lb) * _sigmoid(f_ref[0].astype(F32))
    k_all = 1.0 - f_all
    lf_hi, lf_lo = _split2(jnp.log(f_all))

    sums = [jnp.dot(sm, lf_hi[chunk_rows(c)], preferred_element_type=F32)
            + jnp.dot(sm, lf_lo[chunk_rows(c)], preferred_element_type=F32) for c in range(n_chunks)]

    scaled = []
    for c in range(n_chunks):
        q, k, sc = q_all[chunk_rows(c)], k_all[chunk_rows(c)], sums[c]
        b = sc[:CHUNK]
        b_last = b[CHUNK - 1:CHUNK, :]
        lv = []
        for l, s in enumerate(levels):
            ex = jnp.exp(sc[(l + 1) * CHUNK:(l + 2) * CHUNK])
            upper = (row & (2 * s - 1)) >= s
            lv.append((jnp.where(upper, q * ex, 0.0).astype(BF16), jnp.where(upper, 0.0, k * ex).astype(BF16)))
        scaled.append(dict(q=q.astype(BF16), k=k.astype(BF16), lv=lv, qb=(q * jnp.exp(b)).astype(BF16),
                           kd=(k * jnp.exp(b_last - b)).astype(BF16), f_last=jnp.exp(b_last),
                           v=i_ref[0, chunk_rows(c), :].astype(BF16)))
    items = [(c, j) for c in range(n_chunks) for j in heads]

    a_s = [jnp.where(ri == ci, _mm_nt(scaled[c]["q"][:, head_lanes(j)], scaled[c]["k"][:, head_lanes(j)]), 0.0)
           for c, j in items]
    for l, s in enumerate(levels):
        same = (ri // (2 * s)) == (ci // (2 * s))
        parts = [_mm_nt(scaled[c]["lv"][l][0][:, head_lanes(j)], scaled[c]["lv"][l][1][:, head_lanes(j)])
                 for c, j in items]
        a_s = [a + jnp.where(same, part, 0.0) for a, part in zip(a_s, parts)]

    kv_s = [_mm_tn(scaled[c]["v"][:, head_lanes(j)], scaled[c]["kd"][:, head_lanes(j)]) for c, j in items]
    states = [st_ref[j] for j in heads]
    starts = []
    for idx, (c, j) in enumerate(items):
        starts.append(states[j].astype(BF16))
        states[j] = states[j] * scaled[c]["f_last"][:, head_lanes(j)] + kv_s[idx]
    for j in heads:
        st_ref[j] = states[j]

    inter = [_mm_nt(scaled[c]["qb"][:, head_lanes(j)], st0) for (c, j), st0 in zip(items, starts)]
    o_s = [x + _mm(a, scaled[c]["v"][:, head_lanes(j)]) for x, a, (c, j) in zip(inter, a_s, items)]
    for (c, j), o in zip(items, o_s):
        g = g_ref[0, chunk_rows(c), head_lanes(j)].astype(F32)
        o_ref[0, chunk_rows(c), head_lanes(j)] = _head_rms_gate(o, nw, g).astype(o_ref.dtype)


def _gla_core(proj, lb_logits, layer, norm_w, *, time_tile=512, heads_per_step=4):
    bsz, t, n = proj.shape
    dv = norm_w.shape[0]
    n_heads = n // (4 * LANES)
    time_tile = min(time_tile, t)
    assert dv == LANES and n == 4 * n_heads * LANES and t % time_tile == 0 and n_heads % heads_per_step == 0
    hh = n_heads // heads_per_step
    width = heads_per_step * LANES
    blk = lambda off: pl.BlockSpec((1, time_tile, width), lambda b, h, i: (b, i, off + h))
    sm = jnp.asarray(_gla_sum_matrix(), BF16)
    return pl.pallas_call(
        functools.partial(_gla_kernel, n_chunks=time_tile // CHUNK, heads_per_step=heads_per_step, layer=layer),
        grid=(bsz, hh, t // time_tile),
        in_specs=[blk(0), blk(hh), blk(2 * hh), blk(3 * hh),
                  pl.BlockSpec((lb_logits.shape[0], width), lambda b, h, i: (0, h)),
                  pl.BlockSpec((1, LANES), lambda b, h, i: (0, 0)),
                  pl.BlockSpec(sm.shape, lambda b, h, i: (0, 0))],
        out_specs=pl.BlockSpec((1, time_tile, width), lambda b, h, i: (b, i, h)),
        out_shape=jax.ShapeDtypeStruct((bsz, t, n_heads * dv), BF16),
        scratch_shapes=[pltpu.VMEM((heads_per_step, LANES, LANES), F32)],
        compiler_params=_params("parallel", "parallel", "arbitrary"),
        name="gla_chunk",
    )(proj, proj, proj, proj, lb_logits.astype(F32), norm_w.reshape(1, dv), sm)


def _hgrn2_mixer(h, g, scale, shift, w_in, lb_logits, layer, norm_w):
    proj = _proj(h, g, scale, shift, w_in.astype(BF16), out_dtype=BF16)
    return _gla_core(proj, lb_logits, layer, norm_w)


def _rwkv_proj_kernel(h_ref, g_ref, sc_ref, sh_ref, mu_ref, wrkv_ref, w0_ref, w1_ref, w2_ref, a0_ref, a1_ref,
                      a2_ref, g1_ref, g2_ref, r_ref, k_ref, v_ref, a_ref, wl_ref, gate_ref, carry_ref):
    @pl.when(pl.program_id(1) == 0)
    def _():
        carry_ref[...] = jnp.zeros_like(carry_ref)

    u = _norm_mod(h_ref[0], g_ref[...], sc_ref[0], sh_ref[0])
    dx = _shift_rows(u, carry_ref[...], 1) - u
    carry_ref[...] = u[-SUBLANES:]
    mu = mu_ref[...]
    mix = lambda s: (u + dx * mu[s:s + 1]).astype(BF16)
    dot = lambda a, b: jnp.dot(a, b, preferred_element_type=F32)
    r_ref[0] = dot(mix(0), wrkv_ref[0]).astype(r_ref.dtype)
    k_ref[0] = dot(mix(2), wrkv_ref[1]).astype(k_ref.dtype)
    v_ref[0] = dot(mix(3), wrkv_ref[2]).astype(v_ref.dtype)
    w = w0_ref[...] + dot(jnp.tanh(dot(mix(1), w1_ref[...])).astype(BF16), w2_ref[...])
    w = -_softplus(-w) - 0.5
    wl_ref[0] = -jnp.exp(w)
    a_ref[0] = _sigmoid(a0_ref[...] + dot(dot(mix(4), a1_ref[...]).astype(BF16), a2_ref[...]))
    gate_ref[0] = dot(_sigmoid(dot(mix(5), g1_ref[...])).astype(BF16), g2_ref[...]).astype(gate_ref.dtype)


def _pad_lora(w_in, w_out):
    rank = w_in.shape[1]
    pad = -(-rank // LANES) * LANES - rank
    return (jnp.pad(w_in, ((0, 0), (0, pad))).astype(BF16), jnp.pad(w_out, ((0, pad), (0, 0))).astype(BF16))


def _rwkv_proj(h, g, scale, shift, mu, w_rkv, w0, w1, w2, a0, a1, a2, g1, g2, *, row_tile=256):
    bsz, t, d = h.shape
    row_tile = min(row_tile, t)
    assert t % row_tile == 0
    w1p, w2p = _pad_lora(w1, w2)
    a1p, a2p = _pad_lora(a1, a2)
    g1p, g2p = _pad_lora(g1, g2)
    const2 = lambda b, i: (0, 0)
    per_b = lambda b, i: (b, 0, 0)
    tile = pl.BlockSpec((1, row_tile, d), lambda b, i: (b, i, 0))
    full = lambda a: pl.BlockSpec(a.shape, const2)
    vec = pl.BlockSpec((1, d), const2)
    f32_out = jax.ShapeDtypeStruct((bsz, t, d), F32)
    bf16_out = jax.ShapeDtypeStruct((bsz, t, d), BF16)
    return pl.pallas_call(
        _rwkv_proj_kernel,
        grid=(bsz, t // row_tile),
        in_specs=[tile, vec, pl.BlockSpec((1, 1, d), per_b), pl.BlockSpec((1, 1, d), per_b),
                  full(mu), pl.BlockSpec(w_rkv.shape, lambda b, i: (0, 0, 0)),
                  vec, full(w1p), full(w2p), vec, full(a1p), full(a2p), full(g1p), full(g2p)],
        out_specs=[tile] * 6,
        out_shape=[bf16_out] * 3 + [f32_out] * 2 + [bf16_out],
        scratch_shapes=[pltpu.VMEM((SUBLANES, d), F32)],
        compiler_params=_params("parallel", "arbitrary"),
        name="rwkv_proj",
    )(h, g.reshape(1, d), scale.reshape(bsz, 1, d), shift.reshape(bsz, 1, d), mu, w_rkv.astype(BF16),
      w0.reshape(1, d), w1p, w2p, a0.reshape(1, d), a1p, a2p, g1p, g2p)


def _rwkv_chunk_kernel(r_ref, k_ref, v_ref, a_ref, wl_ref, gate_ref, kk_ref, ka_ref, rk_ref, lnw_ref, lnb_ref,
                       tri_ref, o_ref, s_ref, at_ref, rt_ref, bt_ref, kt_ref, bonus_ref, y_ref, s0_ref,
                       *, n_chunks, pairs_per_step, head_size, gn_eps):
    @pl.when(pl.program_id(2) == 0)
    def _():
        s_ref[...] = jnp.zeros_like(s_ref)

    c_len = CHUNK
    lane = lax.broadcasted_iota(jnp.int32, (1, LANES), 1)
    lo = lane < head_size
    lane_t = lax.broadcasted_iota(jnp.int32, (c_len, LANES), 1) & (head_size - 1)
    row_t = lax.broadcasted_iota(jnp.int32, (c_len, LANES), 0)
    strict = lane_t < row_t
    incl = lane_t <= row_t
    eye2 = jnp.where(lane_t == row_t, 1.0, 0.0)
    bd_r = lax.broadcasted_iota(jnp.int32, (LANES, LANES), 0) < head_size
    bd_c = lax.broadcasted_iota(jnp.int32, (LANES, LANES), 1) < head_size
    block_diag = bd_r == bd_c
    tri = tri_ref[...]

    def pair_sum(x):
        s_lo = jnp.sum(jnp.where(lo, x, 0.0), axis=-1, keepdims=True)
        s_hi = jnp.sum(jnp.where(lo, 0.0, x), axis=-1, keepdims=True)
        return jnp.where(lo, s_lo, s_hi)

    def stack(x, first_lo=True):
        x_lo, x_hi = jnp.where(lo, x, 0.0), jnp.where(lo, 0.0, x)
        return jnp.concatenate([x_lo, x_hi] if first_lo else [x_hi, x_lo], axis=0)

    pairs = range(pairs_per_step)
    lanes_of = lambda p: slice(p * LANES, (p + 1) * LANES)
    kk_all, k_all = [], []
    for p in pairs:
        pl_ = lanes_of(p)
        k_raw = k_ref[0, :, pl_]
        a = a_ref[0, :, pl_]
        kx = k_raw * kk_ref[:, pl_]
        kk_all.append(kx * lax.rsqrt(pair_sum(kx * kx) + 1e-6))
        k_all.append(k_raw * (1.0 + (a - 1.0) * ka_ref[:, pl_]))
        bonus_ref[:, pl_] = pair_sum(r_ref[0, :, pl_] * k_all[p] * rk_ref[:, pl_]) * v_ref[0, :, pl_]

    cums = []
    for c in range(n_chunks):
        wl_hi, wl_lo = _split2(wl_ref[0, pl.ds(c * c_len, c_len), :])
        cums.append(jnp.dot(tri, wl_hi, preferred_element_type=F32)
                    + jnp.dot(tri, wl_lo, preferred_element_type=F32))
    cum_all = jnp.concatenate(cums, axis=0)
    gam_last = [jnp.exp(cm[c_len - 1:c_len, :]) for cm in cums]

    for p in pairs:
        pl_ = lanes_of(p)
        cum = cum_all[:, pl_]
        inv_gam = jnp.exp(-cum)
        at_ref[:, pl_] = (-kk_all[p] * jnp.exp(cum - wl_ref[0, :, pl_])).astype(BF16)
        rt_ref[:, pl_] = (r_ref[0, :, pl_] * jnp.exp(cum)).astype(BF16)
        bt_ref[:, pl_] = (kk_all[p] * a_ref[0, :, pl_] * inv_gam).astype(BF16)
        kt_ref[:, pl_] = (k_all[p] * inv_gam).astype(BF16)

    items = []
    for c, p in [(c, p) for c in range(n_chunks) for p in pairs]:
        rows = pl.ds(c * c_len, c_len)
        pl_ = lanes_of(p)
        a_t = at_ref[rows, pl_]
        r_t = rt_ref[rows, pl_]
        b_t = bt_ref[rows, pl_]
        k_t = kt_ref[rows, pl_]
        v = v_ref[0, rows, pl_]
        bk = jnp.concatenate([b_t, k_t], axis=0)
        ar = jnp.concatenate([a_t, r_t], axis=0)
        zero = jnp.zeros_like(ar)
        r1 = _mm_nt(jnp.where(lo, ar, zero), bk)
        r2 = _mm_nt(jnp.where(lo, zero, ar), jnp.concatenate([k_t, b_t], axis=0))
        a_ab = jnp.where(strict, jnp.where(lo, r1[:c_len], r2[:c_len]), 0.0)
        a_ak = jnp.where(strict, jnp.where(lo, r2[:c_len], r1[:c_len]), 0.0)
        a_rb = jnp.where(incl, jnp.where(lo, r1[c_len:], r2[c_len:]), 0.0)
        a_rk = jnp.where(incl, jnp.where(lo, r2[c_len:], r1[c_len:]), 0.0)
        items.append(dict(
            rows=rows, pl=pl_, pair=p, a_ab=a_ab, a_ak=a_ak.astype(BF16),
            a_r=jnp.concatenate([a_rb, a_rk], axis=1).astype(BF16),
            a_t=a_t, r_t=r_t, bk=bk, v_sw=stack(v, first_lo=False).astype(BF16), v=v,
            gl_row=gam_last[c][:, pl_]))

    ms = [it["a_ab"] for it in items]
    ts = [eye2 + m for m in ms]
    power = 1
    while 2 * power < c_len:
        ms = [_mm(m, stack(m)) for m in ms]
        ts = [t + _mm(t, stack(m)) for t, m in zip(ts, ms)]
        power *= 2

    w_ts = [_mm(t, stack(it["a_t"])).astype(BF16) for t, it in zip(ts, items)]
    akvs = [_mm(it["a_ak"], it["v_sw"]) for it in items]
    u_0s = [_mm(t, stack(akv)) for t, akv in zip(ts, akvs)]

    m_s = [(jnp.where(block_diag, _mm_tn(w_t, it["bk"][:c_len]), 0.0) * it["gl_row"]).astype(BF16)
           for w_t, it in zip(w_ts, items)]
    n_s = [jnp.where(block_diag, _mm_tn(jnp.concatenate([u_0, it["v"]], axis=0), it["bk"]), 0.0) * it["gl_row"]
           for u_0, it in zip(u_0s, items)]

    states = [s_ref[p] for p in pairs]
    for c in range(n_chunks):
        idxs = [c * pairs_per_step + p for p in pairs]
        sbs = [states[p].astype(BF16) for p in pairs]
        for idx, sb in zip(idxs, sbs):
            s0_ref[idx] = sb
        prods = [_mm(sb, m_s[idx]) for idx, sb in zip(idxs, sbs)]
        states = [states[p] * items[idx]["gl_row"] + prods[p] + n_s[idx] for p, idx in zip(pairs, idxs)]
    for p in pairs:
        s_ref[p] = states[p]

    wr_s = [_mm_nt(jnp.concatenate([w_t, it["r_t"]], axis=0), s0_ref[idx])
            for idx, (it, w_t) in enumerate(zip(items, w_ts))]
    y_s = [wr[c_len:] + _mm(it["a_r"], jnp.concatenate([stack(wr[:c_len] + u_0).astype(BF16), it["v_sw"]], axis=0))
           for it, wr, u_0 in zip(items, wr_s, u_0s)]
    for it, y in zip(items, y_s):
        y_ref[it["rows"], it["pl"]] = y

    inv_n = 1.0 / head_size
    for p in pairs:
        pl_ = lanes_of(p)
        y = y_ref[:, pl_]
        mean = pair_sum(y) * inv_n
        yc = y - mean
        var = pair_sum(yc * yc) * inv_n
        y_n = yc * lax.rsqrt(var + gn_eps) * lnw_ref[:, pl_] + lnb_ref[:, pl_]
        o_ref[0, :, pl_] = ((y_n + bonus_ref[:, pl_]) * gate_ref[0, :, pl_].astype(F32)).astype(o_ref.dtype)


def _rwkv_chunk(r, k, v, a, wl, gate, k_k, k_a, r_k, ln_w, ln_b, *, time_tile=512, pairs_per_step=4):
    bsz, t, d = r.shape
    head_size = r_k.shape[1]
    assert 2 * head_size == LANES and d % (LANES * pairs_per_step) == 0
    time_tile = min(time_tile, t)
    assert t % time_tile == 0
    width = LANES * pairs_per_step
    n_pairs = d // width
    blk = pl.BlockSpec((1, time_tile, width), lambda b, p, i: (b, i, p))
    vec = pl.BlockSpec((1, width), lambda b, p, i: (0, p))
    tri = jnp.asarray(np.tril(np.ones((CHUNK, CHUNK), np.float32)), BF16)
    flat = lambda x: x.reshape(1, d).astype(F32)
    return pl.pallas_call(
        functools.partial(_rwkv_chunk_kernel, n_chunks=time_tile // CHUNK, pairs_per_step=pairs_per_step,
                          head_size=head_size, gn_eps=1e-5 * head_size * head_size),
        grid=(bsz, n_pairs, t // time_tile),
        in_specs=[blk] * 6 + [vec] * 5 + [pl.BlockSpec((CHUNK, CHUNK), lambda b, p, i: (0, 0))],
        out_specs=blk,
        out_shape=jax.ShapeDtypeStruct((bsz, t, d), BF16),
        scratch_shapes=[pltpu.VMEM((pairs_per_step, LANES, LANES), F32)]
        + [pltpu.VMEM((time_tile, width), BF16)] * 4 + [pltpu.VMEM((time_tile, width), F32)] * 2
        + [pltpu.VMEM((time_tile // CHUNK * pairs_per_step, LANES, LANES), BF16)],
        compiler_params=_params("parallel", "parallel", "arbitrary"),
        name="rwkv_chunk",
    )(r, k, v, a, wl, gate, flat(k_k), flat(k_a), flat(r_k), flat(ln_w), flat(ln_b), tri)


def _rwkv7_mixer(h, g, scale, shift, mu, w_rkv, w0, w1, w2, a0, a1, a2, g1, g2, k_k, k_a, r_k, ln_w, ln_b):
    r, k, v, a, wl, gt = _rwkv_proj(h, g, scale, shift, mu, w_rkv, w0, w1, w2, a0, a1, a2, g1, g2)
    return _rwkv_chunk(r, k, v, a, wl, gt, k_k, k_a, r_k, ln_w, ln_b)


def kernel(x, c, norm_g, ada_w, ada_b, a_w_in, a_conv, a_A_log, a_dt_bias, a_norm, a_w_out, b_w_in, hgrn_lb_logits, b_norm, b_w_out, c_mu, c_w_rkv, c_w0, c_w1, c_w2, c_a0, c_a1, c_a2, c_g1, c_g2, c_k_k, c_k_a, c_r_k, c_ln_w, c_ln_b, c_w_out, f_w_up, f_conv_w, f_conv_b, f_w_down, final_g):
    depth = ada_w.shape[0]
    n_mixers = 3
    mod = _ada_mod(c, ada_w, ada_b)
    h = x
    for i in range(depth):
        shift1, scale1, gate1, shift2, scale2, gate2 = (mod[i, :, s] for s in range(6))
        kind, j = i % n_mixers, i // n_mixers
        if kind == 0:
            y = _deltanet_mixer(h, norm_g[i, 0], scale1, shift1, a_w_in[j], a_conv[j], a_A_log[j], a_dt_bias[j],
                                a_norm[j])
            w_out = a_w_out[j]
        elif kind == 1:
            y = _hgrn2_mixer(h, norm_g[i, 0], scale1, shift1, b_w_in[j], hgrn_lb_logits, i, b_norm[j])
            w_out = b_w_out[j]
        else:
            y = _rwkv7_mixer(h, norm_g[i, 0], scale1, shift1, c_mu[j], c_w_rkv[j], c_w0[j], c_w1[j], c_w2[j],
                             c_a0[j], c_a1[j], c_a2[j], c_g1[j], c_g2[j], c_k_k[j], c_k_a[j], c_r_k[j], c_ln_w[j],
                             c_ln_b[j])
            w_out = c_w_out[j]
        h = _mix_out_ffn(y, w_out, gate1, h, norm_g[i, 1], scale2, shift2, gate2, f_w_up[i], f_conv_w[i],
                         f_conv_b[i], f_w_down[i], final_g if i == depth - 1 else None)
    return h
```

```python
import functools

import numpy as np
import jax
import jax.numpy as jnp
from jax import lax
from jax.experimental import pallas as pl
from jax.experimental.pallas import tpu as pltpu

F32 = jnp.float32
BF16 = jnp.bfloat16

CHUNK = 64
NORM_EPS = 1e-6
LANES = 128
SUBLANES = 8
VMEM_LIMIT_BYTES = 56 * 1024 * 1024


def _params(*sem):
    return pltpu.CompilerParams(dimension_semantics=sem, vmem_limit_bytes=VMEM_LIMIT_BYTES)


def _mm(a, b):
    return jnp.dot(a.astype(BF16), b.astype(BF16), preferred_element_type=F32)


def _mm_nt(a, b):
    return lax.dot_general(a.astype(BF16), b.astype(BF16), (((1,), (1,)), ((), ())),
                           preferred_element_type=F32)


def _mm_tn(a, b):
    return lax.dot_general(a.astype(BF16), b.astype(BF16), (((0,), (0,)), ((), ())),
                           preferred_element_type=F32)


def _split2(x):
    hi = x.astype(BF16)
    lo = (x - hi.astype(F32)).astype(BF16)
    return hi, lo


def _sigmoid(x):
    return 1.0 / (1.0 + jnp.exp(-x))


def _silu(x):
    return x * _sigmoid(x)


def _softplus(x):
    return jnp.maximum(x, 0.0) + jnp.log(1.0 + jnp.exp(-jnp.abs(x)))


def _norm_mod(h, g, scale, shift):
    ms = jnp.mean(h * h, axis=-1, keepdims=True)
    y = h * lax.rsqrt(ms + NORM_EPS)
    return (y * g) * (1.0 + scale) + shift


def _shift_rows(x, prev8, s):
    rows, cols = x.shape
    rot = pltpu.roll(x.reshape(rows // SUBLANES, SUBLANES, cols), s, axis=1)
    prev = jnp.concatenate([pltpu.roll(prev8, s, axis=0)[None], rot[:-1]], axis=0)
    sub = lax.broadcasted_iota(jnp.int32, (1, SUBLANES, cols), 1)
    return jnp.where(sub < s, prev, rot).reshape(rows, cols)


def _mod_kernel(c_ref, w_ref, b_ref, o_ref):
    c = c_ref[...]
    o_ref[0] = _mm(_silu(c), w_ref[0]) + b_ref[0]


def _ada_mod(c, ada_w, ada_b):
    depth, d, d6 = ada_w.shape
    bsz = c.shape[0]
    rows = -(-bsz // SUBLANES) * SUBLANES
    c_pad = jnp.zeros((rows, d), F32).at[:bsz].set(c)
    out = pl.pallas_call(
        _mod_kernel,
        grid=(depth, d6 // d),
        in_specs=[pl.BlockSpec((rows, d), lambda i, j: (0, 0)),
                  pl.BlockSpec((1, d, d), lambda i, j: (i, 0, j)),
                  pl.BlockSpec((1, 1, d), lambda i, j: (i, 0, j))],
        out_specs=pl.BlockSpec((1, rows, d), lambda i, j: (i, 0, j)),
        out_shape=jax.ShapeDtypeStruct((depth, rows, d6), F32),
        compiler_params=_params("parallel", "parallel"),
        name="ada_mod",
    )(c_pad, ada_w, ada_b.reshape(depth, 1, d6))
    return out[:, :bsz].reshape(depth, bsz, d6 // d, d)


def _proj_kernel(h_ref, g_ref, sc_ref, sh_ref, w_ref, o_ref, *, n_col_tiles, col_tile):
    ub = _norm_mod(h_ref[0], g_ref[...], sc_ref[0], sh_ref[0]).astype(BF16)
    for j in range(n_col_tiles):
        sl = slice(j * col_tile, (j + 1) * col_tile)
        o_ref[0, :, sl] = jnp.dot(ub, w_ref[:, sl], preferred_element_type=F32).astype(o_ref.dtype)


def _proj(h, g, scale, shift, w, *, row_tile=512, col_tile=1024, out_dtype=BF16):
    bsz, t, d = h.shape
    n = w.shape[1]
    col_tile = min(col_tile, n)
    row_tile = min(row_tile, t)
    assert t % row_tile == 0 and n % col_tile == 0
    return pl.pallas_call(
        functools.partial(_proj_kernel, n_col_tiles=n // col_tile, col_tile=col_tile),
        grid=(bsz, t // row_tile),
        in_specs=[pl.BlockSpec((1, row_tile, d), lambda b, i: (b, i, 0)),
                  pl.BlockSpec((1, d), lambda b, i: (0, 0)),
                  pl.BlockSpec((1, 1, d), lambda b, i: (b, 0, 0)),
                  pl.BlockSpec((1, 1, d), lambda b, i: (b, 0, 0)),
                  pl.BlockSpec((d, n), lambda b, i: (0, 0), pipeline_mode=pl.Buffered(1))],
        out_specs=pl.BlockSpec((1, row_tile, n), lambda b, i: (b, i, 0)),
        out_shape=jax.ShapeDtypeStruct((bsz, t, n), out_dtype),
        compiler_params=_params("parallel", "parallel"),
        name="norm_proj",
    )(h, g.reshape(1, d), scale.reshape(bsz, 1, d), shift.reshape(bsz, 1, d), w)


def _dn_proj_kernel(h_ref, g_ref, sc_ref, sh_ref, w_ref, wst_ref, cw_ref, o_ref, ost_ref, carry_ref, *,
                    n_conv_cols, head_dim, col_tile):
    @pl.when(pl.program_id(1) == 0)
    def _():
        carry_ref[...] = jnp.zeros_like(carry_ref)

    u = _norm_mod(h_ref[0], g_ref[...], sc_ref[0], sh_ref[0])
    ub = u.astype(BF16)
    n_total = o_ref.shape[2]
    key_cols = n_conv_cols // 3
    n_tiles = n_total // col_tile
    tile_dot = lambda j: jnp.dot(ub, w_ref[:, j * col_tile:(j + 1) * col_tile], preferred_element_type=F32)
    x_next = tile_dot(0)
    for j in range(n_tiles):
        sl = slice(j * col_tile, (j + 1) * col_tile)
        x = x_next
        if j + 1 < n_tiles:
            x_next = tile_dot(j + 1)
        if j * col_tile < n_conv_cols:
            prev8 = carry_ref[:, sl]
            cw = cw_ref[:, sl]
            y = _silu(_shift_rows(x, prev8, 3) * cw[0:1] + _shift_rows(x, prev8, 2) * cw[1:2]
                      + _shift_rows(x, prev8, 1) * cw[2:3] + x * cw[3:4])
            carry_ref[:, sl] = x[-SUBLANES:]
            if j * col_tile < 2 * key_cols:
                post = head_dim ** -0.5 if j * col_tile < key_cols else 1.0
                for hd in range(col_tile // head_dim):
                    hl = slice(hd * head_dim, (hd + 1) * head_dim)
                    yh = y[:, hl]
                    inv = lax.rsqrt(jnp.sum(yh * yh, axis=-1, keepdims=True) + 1e-6) * post
                    o_ref[0, :, j * col_tile + hd * head_dim:j * col_tile + (hd + 1) * head_dim] = (
                        yh * inv).astype(o_ref.dtype)
            else:
                o_ref[0, :, sl] = y.astype(o_ref.dtype)
        else:
            o_ref[0, :, sl] = x.astype(o_ref.dtype)
    u_hi, u_lo = _split2(u)
    w_hi, w_lo = _split2(wst_ref[...])
    ost_ref[0] = _mm_nt(w_hi, u_hi) + _mm_nt(w_hi, u_lo) + _mm_nt(w_lo, u_hi)


def _dn_proj(h, g, scale, shift, w, w_small_t, conv_w, head_dim, *, row_tile=512, col_tile=1024):
    bsz, t, d = h.shape
    n = w.shape[1]
    n_conv = conv_w.shape[1]
    row_tile = min(row_tile, t)
    r = w_small_t.shape[0]
    assert t % row_tile == 0 and n % col_tile == 0 and n_conv % col_tile == 0 and (n_conv // 3) % col_tile == 0
    const = lambda b, i: (0, 0)
    per_b = lambda b, i: (b, 0, 0)
    return pl.pallas_call(
        functools.partial(_dn_proj_kernel, n_conv_cols=n_conv, head_dim=head_dim, col_tile=col_tile),
        grid=(bsz, t // row_tile),
        in_specs=[pl.BlockSpec((1, row_tile, d), lambda b, i: (b, i, 0)),
                  pl.BlockSpec((1, d), const),
                  pl.BlockSpec((1, 1, d), per_b),
                  pl.BlockSpec((1, 1, d), per_b),
                  pl.BlockSpec((d, n), const, pipeline_mode=pl.Buffered(1)),
                  pl.BlockSpec((r, d), const),
                  pl.BlockSpec(conv_w.shape, const)],
        out_specs=[pl.BlockSpec((1, row_tile, n), lambda b, i: (b, i, 0)),
                   pl.BlockSpec((1, r, row_tile), lambda b, i: (b, 0, i))],
        out_shape=[jax.ShapeDtypeStruct((bsz, t, n), BF16), jax.ShapeDtypeStruct((bsz, r, t), F32)],
        scratch_shapes=[pltpu.VMEM((SUBLANES, n_conv), F32)],
        compiler_params=_params("parallel", "arbitrary"),
        name="deltanet_proj",
    )(h, g.reshape(1, d), scale.reshape(bsz, 1, d), shift.reshape(bsz, 1, d), w, w_small_t, conv_w)


def _ffn_kernel(x_ref, wo_ref, gate1_ref, h_ref, g_ref, sc_ref, sh_ref, gate_ref, wv_ref, wg_ref, cw_ref, cb_ref,
                wd_ref, fg_ref, o_ref, act_ref, carry_ref, *, d_ff, col_tile, final_norm):
    i = pl.program_id(1)
    h = h_ref[0] + gate1_ref[0] * jnp.dot(x_ref[0], wo_ref[...], preferred_element_type=F32)
    ub = _norm_mod(h, g_ref[...], sc_ref[0], sh_ref[0]).astype(BF16)

    @pl.when(i == 0)
    def _():
        carry_ref[...] = jnp.zeros_like(carry_ref)

    def conv(up, prev8, w3, bias):
        return (_shift_rows(up, prev8, 2) * w3[0:1] + _shift_rows(up, prev8, 1) * w3[1:2]
                + up * w3[2:3] + bias)

    n_tiles = d_ff // col_tile

    def up_proj(j):
        sl = slice(j * col_tile, (j + 1) * col_tile)
        return (jnp.dot(ub, wv_ref[:, sl], preferred_element_type=F32),
                jnp.dot(ub, wg_ref[:, sl], preferred_element_type=F32))

    nxt = up_proj(0)
    for j in range(n_tiles):
        sl = slice(j * col_tile, (j + 1) * col_tile)
        sl_g = slice(d_ff + j * col_tile, d_ff + (j + 1) * col_tile)
        up_v, up_g = nxt
        if j + 1 < n_tiles:
            nxt = up_proj(j + 1)
        val = conv(up_v, carry_ref[:, sl], cw_ref[:, sl], cb_ref[:, sl])
        gat = conv(up_g, carry_ref[:, sl_g], cw_ref[:, sl_g], cb_ref[:, sl_g])
        carry_ref[:, sl] = up_v[-SUBLANES:]
        carry_ref[:, sl_g] = up_g[-SUBLANES:]
        act_ref[:, sl] = (val * _silu(gat)).astype(BF16)

    y = jnp.dot(act_ref[...], wd_ref[...], preferred_element_type=F32)
    out = h + gate_ref[0] * y
    if final_norm:
        ms = jnp.mean(out * out, axis=-1, keepdims=True)
        out = out * lax.rsqrt(ms + NORM_EPS) * fg_ref[...]
    o_ref[0] = out


def _mix_out_ffn(x, w_out, gate1, h, g, scale, shift, gate, w_up_all, conv_w, conv_b, w_down_all, layer,
                 final_g=None, *, row_tile=512, col_tile=256):
    bsz, t, d = h.shape
    k_in = x.shape[2]
    d_ff = w_down_all.shape[1]
    row_tile = min(row_tile, t)
    assert t % row_tile == 0 and d_ff % col_tile == 0
    final_norm = final_g is not None
    fg = (final_g if final_norm else jnp.ones((d,), F32)).reshape(1, d)
    const = lambda b, i: (0, 0)
    per_b = lambda b, i: (b, 0, 0)
    tile = lambda width: pl.BlockSpec((1, row_tile, width), lambda b, i: (b, i, 0))
    resident = lambda shape: pl.BlockSpec(shape, const, pipeline_mode=pl.Buffered(1))
    layer_block = lambda shape, col: pl.BlockSpec((None,) + shape, lambda b, i: (layer, 0, col),
                                                  pipeline_mode=pl.Buffered(1))
    vec = lambda: pl.BlockSpec((1, 1, d), per_b)
    return pl.pallas_call(
        functools.partial(_ffn_kernel, d_ff=d_ff, col_tile=col_tile, final_norm=final_norm),
        grid=(bsz, t // row_tile),
        in_specs=[tile(k_in), resident((k_in, d)), vec(),
                  tile(d), pl.BlockSpec((1, d), const), vec(), vec(), vec(),
                  layer_block((d, d_ff), 0), layer_block((d, d_ff), 1),
                  pl.BlockSpec((3, 2 * d_ff), const),
                  pl.BlockSpec((1, 2 * d_ff), const),
                  layer_block((d_ff, d), 0),
                  pl.BlockSpec((1, d), const)],
        out_specs=tile(d),
        out_shape=jax.ShapeDtypeStruct((bsz, t, d), F32),
        scratch_shapes=[pltpu.VMEM((row_tile, d_ff), BF16),
                        pltpu.VMEM((SUBLANES, 2 * d_ff), F32)],
        compiler_params=_params("parallel", "arbitrary"),
        name="mix_out_conv_glu_ffn",
    )(x, w_out.astype(BF16), gate1.reshape(bsz, 1, d),
      h, g.reshape(1, d), scale.reshape(bsz, 1, d), shift.reshape(bsz, 1, d), gate.reshape(bsz, 1, d),
      w_up_all, w_up_all, conv_w, conv_b.reshape(1, 2 * d_ff), w_down_all, fg)


def _unit_lower_inverse(neg_l):
    c = neg_l.shape[0]
    ri = lax.broadcasted_iota(jnp.int32, (c, c), 0)
    ci = lax.broadcasted_iota(jnp.int32, (c, c), 1)
    p = jnp.where(ri == ci, 1.0, 0.0) + neg_l
    m = neg_l
    power = 1
    while 2 * power < c:
        m = _mm(m, m)
        p = p + _mm(p, m)
        power *= 2
    return p


def _head_rms_gate(o, norm_w, z):
    ms = jnp.mean(o * o, axis=-1, keepdims=True)
    return (o * lax.rsqrt(ms + NORM_EPS) * norm_w) * _silu(z)


def _dn_kernel(q_ref, k_ref, v_ref, z_ref, abt_ref, alog_ref, dtb_ref, nw_ref,
               o_ref, s_ref, s0_ref, *, n_chunks, n_heads, heads_per_step):
    head0 = pl.program_id(1) * heads_per_step

    @pl.when(pl.program_id(2) == 0)
    def _():
        s_ref[...] = jnp.zeros_like(s_ref)

    ri = lax.broadcasted_iota(jnp.int32, (CHUNK, CHUNK), 0)
    ci = lax.broadcasted_iota(jnp.int32, (CHUNK, CHUNK), 1)
    nw = nw_ref[...]

    gates = []
    for j in range(heads_per_step):
        a_row = abt_ref[0, pl.ds(head0 + j, 1), :]
        b_row = abt_ref[0, pl.ds(n_heads + head0 + j, 1), :]
        g_all = -jnp.exp(alog_ref[j][:, :1]) * _softplus(a_row + dtb_ref[j][:, :1])
        gates.append((g_all, _sigmoid(b_row)))

    chunks = []
    for c, j in [(c, j) for c in range(n_chunks) for j in range(heads_per_step)]:
        rows = pl.ds(c * CHUNK, CHUNK)
        lanes = slice(c * CHUNK, (c + 1) * CHUNK)
        hl = slice(j * LANES, (j + 1) * LANES)
        g_all, beta_all = gates[j]
        q = q_ref[0, rows, hl].astype(F32)
        k = k_ref[0, rows, hl].astype(F32)
        v = v_ref[0, rows, hl].astype(F32)
        g_row = g_all[:, lanes]
        g_col = jnp.sum(jnp.where(ci <= ri, g_row, 0.0), axis=1, keepdims=True)
        g_rowc = jnp.sum(jnp.where(ci == ri, g_col, 0.0), axis=0, keepdims=True)
        beta_col = jnp.sum(jnp.where(ci == ri, beta_all[:, lanes], 0.0), axis=1, keepdims=True)
        g_last = g_col[CHUNK - 1:CHUNK, :]
        decay = jnp.where(ci <= ri, jnp.exp(jnp.minimum(g_col - g_rowc, 0.0)), 0.0)
        e_g = jnp.exp(g_col)
        kb = k * beta_col
        chunks.append(dict(
            rows=rows, hl=hl, head=j, neg_l=jnp.where(ci < ri, -(_mm_nt(kb, k) * decay), 0.0),
            kbe=(kb * e_g).astype(BF16), vb=(v * beta_col).astype(BF16),
            attn=(_mm_nt(q, k) * decay).astype(BF16), qe=(q * e_g).astype(BF16),
            kd=(k * jnp.exp(g_last - g_col)).astype(BF16), gl=jnp.exp(g_last)))

    eye = jnp.where(ri == ci, 1.0, 0.0)
    ms = [ch["neg_l"] for ch in chunks]
    ps = [eye + m for m in ms]
    power = 1
    while 2 * power < CHUNK:
        ms = [_mm(m, m) for m in ms]
        ps = [p + _mm(p, m) for p, m in zip(ps, ms)]
        power *= 2

    ws = [_mm(p, ch["kbe"]).astype(BF16) for p, ch in zip(ps, chunks)]
    us = [_mm(p, ch["vb"]) for p, ch in zip(ps, chunks)]

    a_s = [(-_mm_tn(w, ch["kd"])).astype(BF16) for w, ch in zip(ws, chunks)]
    n_s = [_mm_tn(u, ch["kd"]) for u, ch in zip(us, chunks)]

    heads = range(heads_per_step)
    states = [s_ref[j] for j in heads]
    for c in range(n_chunks):
        idxs = [c * heads_per_step + j for j in heads]
        sbs = [states[j].astype(BF16) for j in heads]
        for idx, sb in zip(idxs, sbs):
            s0_ref[idx] = sb
        prods = [_mm(sb, a_s[idx]) for idx, sb in zip(idxs, sbs)]
        states = [states[j] * chunks[idx]["gl"] + prods[j] + n_s[idx] for j, idx in zip(heads, idxs)]
    for j in heads:
        s_ref[j] = states[j]

    wq_s = [_mm_nt(jnp.concatenate([w, ch["qe"]], axis=0), s0_ref[idx])
            for idx, (w, ch) in enumerate(zip(ws, chunks))]
    o_s = [wq[CHUNK:] + _mm(ch["attn"], u - wq[:CHUNK]) for wq, u, ch in zip(wq_s, us, chunks)]
    for ch, o in zip(chunks, o_s):
        z = z_ref[0, ch["rows"], ch["hl"]].astype(F32)
        o_ref[0, ch["rows"], ch["hl"]] = _head_rms_gate(o, nw, z).astype(o_ref.dtype)


def _deltanet_core(proj, abt, a_log, dt_bias, norm_w, *, time_tile=512, heads_per_step=8):
    bsz, t, _ = proj.shape
    n_heads = a_log.shape[0]
    dv = norm_w.shape[0]
    time_tile = min(time_tile, t)
    assert dv == LANES and proj.shape[2] == 4 * n_heads * LANES and t % time_tile == 0
    assert n_heads % heads_per_step == 0
    hh = n_heads // heads_per_step
    width = heads_per_step * LANES
    blk = lambda off: pl.BlockSpec((1, time_tile, width), lambda b, h, i: (b, i, off + h))
    per_head = pl.BlockSpec((heads_per_step, 1, LANES), lambda b, h, i: (h, 0, 0))
    bcast = lambda a: jnp.broadcast_to(a.astype(F32)[:, None, None], (n_heads, 1, LANES))
    return pl.pallas_call(
        functools.partial(_dn_kernel, n_chunks=time_tile // CHUNK, n_heads=n_heads, heads_per_step=heads_per_step),
        grid=(bsz, hh, t // time_tile),
        in_specs=[blk(0), blk(hh), blk(2 * hh), blk(3 * hh),
                  pl.BlockSpec((1, 2 * n_heads, time_tile), lambda b, h, i: (b, 0, i)),
                  per_head, per_head,
                  pl.BlockSpec((1, LANES), lambda b, h, i: (0, 0))],
        out_specs=pl.BlockSpec((1, time_tile, width), lambda b, h, i: (b, i, h)),
        out_shape=jax.ShapeDtypeStruct((bsz, t, n_heads * dv), BF16),
        scratch_shapes=[pltpu.VMEM((heads_per_step, LANES, LANES), F32),
                        pltpu.VMEM((time_tile // CHUNK * heads_per_step, LANES, LANES), BF16)],
        compiler_params=_params("parallel", "parallel", "arbitrary"),
        name="deltanet_chunk",
    )(proj, proj, proj, proj, abt, bcast(a_log), bcast(dt_bias), norm_w.reshape(1, dv))


def _deltanet_mixer(h, g, scale, shift, w_in, conv_w, a_log, dt_bias, norm_w):
    n_heads = a_log.shape[0]
    n_main = w_in.shape[1] - 2 * n_heads
    proj, abt = _dn_proj(h, g, scale, shift, w_in[:, :n_main].astype(BF16), w_in[:, n_main:].T, conv_w,
                         norm_w.shape[0])
    return _deltanet_core(proj, abt, a_log, dt_bias, norm_w)


def _gla_levels():
    levels, s = [], CHUNK // 2
    while s >= 1:
        levels.append(s)
        s //= 2
    return levels


def _gla_sum_matrix():
    c = CHUNK
    mats = [np.tril(np.ones((c, c), np.float32))]
    for s in _gla_levels():
        m_l = np.zeros((c, c), np.float32)
        for i in range(c):
            mid = (i // (2 * s)) * 2 * s + s
            if i >= mid:
                m_l[i, mid:i + 1] = 1.0
            else:
                m_l[i, i + 1:mid] = 1.0
        mats.append(m_l)
    return np.concatenate(mats, axis=0)


def _gla_kernel(q_ref, f_ref, i_ref, g_ref, lbl_ref, nw_ref, sm_ref, o_ref, st_ref, *, n_chunks, heads_per_step,
                layer):
    @pl.when(pl.program_id(2) == 0)
    def _():
        st_ref[...] = jnp.zeros_like(st_ref)

    logits = lbl_ref[...]
    e = jnp.exp(logits - jnp.max(logits, axis=0, keepdims=True))
    if layer == 0:
        lb = jnp.zeros((1, logits.shape[1]), F32)
    else:
        lb = jnp.sum(e[1:layer + 1], axis=0, keepdims=True) / jnp.sum(e, axis=0, keepdims=True)
    nw = nw_ref[...]
    sm = sm_ref[...]
    ri = lax.broadcasted_iota(jnp.int32, (CHUNK, CHUNK), 0)
    ci = lax.broadcasted_iota(jnp.int32, (CHUNK, CHUNK), 1)
    row = lax.broadcasted_iota(jnp.int32, (CHUNK, 1), 0)
    levels = _gla_levels()
    heads = range(heads_per_step)
    chunk_rows = lambda c: slice(c * CHUNK, (c + 1) * CHUNK)
    head_lanes = lambda j: slice(j * LANES, (j + 1) * LANES)

    q_all = _silu(q_ref[0].astype(F32))
    f_all = lb + (1.0 - lb) * _sigmoid(f_ref[0].astype(F32))
    k_all = 1.0 - f_all
    lf_hi, lf_lo = _split2(jnp.log(f_all))

    sums = [jnp.dot(sm, lf_hi[chunk_rows(c)], preferred_element_type=F32)
            + jnp.dot(sm, lf_lo[chunk_rows(c)], preferred_element_type=F32) for c in range(n_chunks)]

    scaled = []
    for c in range(n_chunks):
        q, k, sc = q_all[chunk_rows(c)], k_all[chunk_rows(c)], sums[c]
        b = sc[:CHUNK]
        b_last = b[CHUNK - 1:CHUNK, :]
        lv = []
        for l, s in enumerate(levels):
            ex = jnp.exp(sc[(l + 1) * CHUNK:(l + 2) * CHUNK])
            upper = (row & (2 * s - 1)) >= s
            lv.append((jnp.where(upper, q * ex, 0.0).astype(BF16), jnp.where(upper, 0.0, k * ex).astype(BF16)))
        scaled.append(dict(q=q.astype(BF16), k=k.astype(BF16), lv=lv, qb=(q * jnp.exp(b)).astype(BF16),
                           kd=(k * jnp.exp(b_last - b)).astype(BF16), f_last=jnp.exp(b_last),
                           v=i_ref[0, chunk_rows(c), :].astype(BF16)))
    items = [(c, j) for c in range(n_chunks) for j in heads]

    a_s = [jnp.where(ri == ci, _mm_nt(scaled[c]["q"][:, head_lanes(j)], scaled[c]["k"][:, head_lanes(j)]), 0.0)
           for c, j in items]
    for l, s in enumerate(levels):
        same = (ri // (2 * s)) == (ci // (2 * s))
        parts = [_mm_nt(scaled[c]["lv"][l][0][:, head_lanes(j)], scaled[c]["lv"][l][1][:, head_lanes(j)])
                 for c, j in items]
        a_s = [a + jnp.where(same, part, 0.0) for a, part in zip(a_s, parts)]

    kv_s = [_mm_tn(scaled[c]["v"][:, head_lanes(j)], scaled[c]["kd"][:, head_lanes(j)]) for c, j in items]
    states = [st_ref[j] for j in heads]
    starts = []
    for idx, (c, j) in enumerate(items):
        starts.append(states[j].astype(BF16))
        states[j] = states[j] * scaled[c]["f_last"][:, head_lanes(j)] + kv_s[idx]
    for j in heads:
        st_ref[j] = states[j]

    inter = [_mm_nt(scaled[c]["qb"][:, head_lanes(j)], st0) for (c, j), st0 in zip(items, starts)]
    o_s = [x + _mm(a, scaled[c]["v"][:, head_lanes(j)]) for x, a, (c, j) in zip(inter, a_s, items)]
    for (c, j), o in zip(items, o_s):
        g = g_ref[0, chunk_rows(c), head_lanes(j)].astype(F32)
        o_ref[0, chunk_rows(c), head_lanes(j)] = _head_rms_gate(o, nw, g).astype(o_ref.dtype)


def _gla_core(proj, lb_logits, layer, norm_w, *, time_tile=512, heads_per_step=4):
    bsz, t, n = proj.shape
    dv = norm_w.shape[0]
    n_heads = n // (4 * LANES)
    time_tile = min(time_tile, t)
    assert dv == LANES and n == 4 * n_heads * LANES and t % time_tile == 0 and n_heads % heads_per_step == 0
    hh = n_heads // heads_per_step
    width = heads_per_step * LANES
    blk = lambda off: pl.BlockSpec((1, time_tile, width), lambda b, h, i: (b, i, off + h))
    sm = jnp.asarray(_gla_sum_matrix(), BF16)
    return pl.pallas_call(
        functools.partial(_gla_kernel, n_chunks=time_tile // CHUNK, heads_per_step=heads_per_step, layer=layer),
        grid=(bsz, hh, t // time_tile),
        in_specs=[blk(0), blk(hh), blk(2 * hh), blk(3 * hh),
                  pl.BlockSpec((lb_logits.shape[0], width), lambda b, h, i: (0, h)),
                  pl.BlockSpec((1, LANES), lambda b, h, i: (0, 0)),
                  pl.BlockSpec(sm.shape, lambda b, h, i: (0, 0))],
        out_specs=pl.BlockSpec((1, time_tile, width), lambda b, h, i: (b, i, h)),
        out_shape=jax.ShapeDtypeStruct((bsz, t, n_heads * dv), BF16),
        scratch_shapes=[pltpu.VMEM((heads_per_step, LANES, LANES), F32)],
        compiler_params=_params("parallel", "parallel", "arbitrary"),
        name="gla_chunk",
    )(proj, proj, proj, proj, lb_logits.astype(F32), norm_w.reshape(1, dv), sm)


def _hgrn2_mixer(h, g, scale, shift, w_in, lb_logits, layer, norm_w):
    proj = _proj(h, g, scale, shift, w_in.astype(BF16))
    return _gla_core(proj, lb_logits, layer, norm_w)


def _rwkv_proj_kernel(h_ref, g_ref, sc_ref, sh_ref, mu_ref, wrkv_ref, w0_ref, w1_ref, w2_ref, a0_ref, a1_ref,
                      a2_ref, g1_ref, g2_ref, r_ref, k_ref, v_ref, a_ref, wl_ref, gate_ref, carry_ref):
    @pl.when(pl.program_id(1) == 0)
    def _():
        carry_ref[...] = jnp.zeros_like(carry_ref)

    u = _norm_mod(h_ref[0], g_ref[...], sc_ref[0], sh_ref[0])
    dx = _shift_rows(u, carry_ref[...], 1) - u
    carry_ref[...] = u[-SUBLANES:]
    mu = mu_ref[...]
    mix = lambda s: (u + dx * mu[s:s + 1]).astype(BF16)
    dot = lambda a, b: jnp.dot(a, b, preferred_element_type=F32)
    r_ref[0] = dot(mix(0), wrkv_ref[0]).astype(r_ref.dtype)
    k_ref[0] = dot(mix(2), wrkv_ref[1]).astype(k_ref.dtype)
    v_ref[0] = dot(mix(3), wrkv_ref[2]).astype(v_ref.dtype)
    w = w0_ref[...] + dot(jnp.tanh(dot(mix(1), w1_ref[...])).astype(BF16), w2_ref[...])
    w = -_softplus(-w) - 0.5
    wl_ref[0] = -jnp.exp(w)
    a_ref[0] = _sigmoid(a0_ref[...] + dot(dot(mix(4), a1_ref[...]).astype(BF16), a2_ref[...]))
    gate_ref[0] = dot(_sigmoid(dot(mix(5), g1_ref[...])).astype(BF16), g2_ref[...]).astype(gate_ref.dtype)


def _pad_lora(w_in, w_out):
    rank = w_in.shape[1]
    pad = -(-rank // LANES) * LANES - rank
    return (jnp.pad(w_in, ((0, 0), (0, pad))).astype(BF16), jnp.pad(w_out, ((0, pad), (0, 0))).astype(BF16))


def _rwkv_proj(h, g, scale, shift, mu, w_rkv, w0, w1, w2, a0, a1, a2, g1, g2, *, row_tile=256):
    bsz, t, d = h.shape
    row_tile = min(row_tile, t)
    assert t % row_tile == 0
    w1p, w2p = _pad_lora(w1, w2)
    a1p, a2p = _pad_lora(a1, a2)
    g1p, g2p = _pad_lora(g1, g2)
    const2 = lambda b, i: (0, 0)
    per_b = lambda b, i: (b, 0, 0)
    tile = pl.BlockSpec((1, row_tile, d), lambda b, i: (b, i, 0))
    full = lambda a: pl.BlockSpec(a.shape, const2)
    vec = pl.BlockSpec((1, d), const2)
    f32_out = jax.ShapeDtypeStruct((bsz, t, d), F32)
    bf16_out = jax.ShapeDtypeStruct((bsz, t, d), BF16)
    return pl.pallas_call(
        _rwkv_proj_kernel,
        grid=(bsz, t // row_tile),
        in_specs=[tile, vec, pl.BlockSpec((1, 1, d), per_b), pl.BlockSpec((1, 1, d), per_b),
                  full(mu), pl.BlockSpec(w_rkv.shape, lambda b, i: (0, 0, 0)),
                  vec, full(w1p), full(w2p), vec, full(a1p), full(a2p), full(g1p), full(g2p)],
        out_specs=[tile] * 6,
        out_shape=[bf16_out] * 3 + [f32_out] * 2 + [bf16_out],
        scratch_shapes=[pltpu.VMEM((SUBLANES, d), F32)],
        compiler_params=_params("parallel", "arbitrary"),
        name="rwkv_proj",
    )(h, g.reshape(1, d), scale.reshape(bsz, 1, d), shift.reshape(bsz, 1, d), mu, w_rkv.astype(BF16),
      w0.reshape(1, d), w1p, w2p, a0.reshape(1, d), a1p, a2p, g1p, g2p)


def _rwkv_chunk_kernel(r_ref, k_ref, v_ref, a_ref, wl_ref, gate_ref, kk_ref, ka_ref, rk_ref, lnw_ref, lnb_ref,
                       tri_ref, o_ref, s_ref, at_ref, rt_ref, bt_ref, kt_ref, bonus_ref, y_ref, s0_ref,
                       *, n_chunks, pairs_per_step, head_size, gn_eps):
    @pl.when(pl.program_id(2) == 0)
    def _():
        s_ref[...] = jnp.zeros_like(s_ref)

    c_len = CHUNK
    lane = lax.broadcasted_iota(jnp.int32, (1, LANES), 1)
    lo = lane < head_size
    lane_t = lax.broadcasted_iota(jnp.int32, (c_len, LANES), 1) & (head_size - 1)
    row_t = lax.broadcasted_iota(jnp.int32, (c_len, LANES), 0)
    strict = lane_t < row_t
    incl = lane_t <= row_t
    eye2 = jnp.where(lane_t == row_t, 1.0, 0.0)
    bd_r = lax.broadcasted_iota(jnp.int32, (LANES, LANES), 0) < head_size
    bd_c = lax.broadcasted_iota(jnp.int32, (LANES, LANES), 1) < head_size
    block_diag = bd_r == bd_c
    tri = tri_ref[...]

    def pair_sum(x):
        s_lo = jnp.sum(jnp.where(lo, x, 0.0), axis=-1, keepdims=True)
        s_hi = jnp.sum(jnp.where(lo, 0.0, x), axis=-1, keepdims=True)
        return jnp.where(lo, s_lo, s_hi)

    def stack(x, first_lo=True):
        x_lo, x_hi = jnp.where(lo, x, 0.0), jnp.where(lo, 0.0, x)
        return jnp.concatenate([x_lo, x_hi] if first_lo else [x_hi, x_lo], axis=0)

    pairs = range(pairs_per_step)
    lanes_of = lambda p: slice(p * LANES, (p + 1) * LANES)
    kk_all, k_all = [], []
    for p in pairs:
        pl_ = lanes_of(p)
        k_raw = k_ref[0, :, pl_]
        a = a_ref[0, :, pl_]
        kx = k_raw * kk_ref[:, pl_]
        kk_all.append(kx * lax.rsqrt(pair_sum(kx * kx) + 1e-6))
        k_all.append(k_raw * (1.0 + (a - 1.0) * ka_ref[:, pl_]))
        bonus_ref[:, pl_] = pair_sum(r_ref[0, :, pl_] * k_all[p] * rk_ref[:, pl_]) * v_ref[0, :, pl_]

    cums = []
    for c in range(n_chunks):
        wl_hi, wl_lo = _split2(wl_ref[0, pl.ds(c * c_len, c_len), :])
        cums.append(jnp.dot(tri, wl_hi, preferred_element_type=F32)
                    + jnp.dot(tri, wl_lo, preferred_element_type=F32))
    cum_all = jnp.concatenate(cums, axis=0)
    gam_last = [jnp.exp(cm[c_len - 1:c_len, :]) for cm in cums]

    for p in pairs:
        pl_ = lanes_of(p)
        cum = cum_all[:, pl_]
        inv_gam = jnp.exp(-cum)
        at_ref[:, pl_] = (-kk_all[p] * jnp.exp(cum - wl_ref[0, :, pl_])).astype(BF16)
        rt_ref[:, pl_] = (r_ref[0, :, pl_] * jnp.exp(cum)).astype(BF16)
        bt_ref[:, pl_] = (kk_all[p] * a_ref[0, :, pl_] * inv_gam).astype(BF16)
        kt_ref[:, pl_] = (k_all[p] * inv_gam).astype(BF16)

    items = []
    for c, p in [(c, p) for c in range(n_chunks) for p in pairs]:
        rows = pl.ds(c * c_len, c_len)
        pl_ = lanes_of(p)
        a_t = at_ref[rows, pl_]
        r_t = rt_ref[rows, pl_]
        b_t = bt_ref[rows, pl_]
        k_t = kt_ref[rows, pl_]
        v = v_ref[0, rows, pl_]
        bk = jnp.concatenate([b_t, k_t], axis=0)
        ar = jnp.concatenate([a_t, r_t], axis=0)
        zero = jnp.zeros_like(ar)
        r1 = _mm_nt(jnp.where(lo, ar, zero), bk)
        r2 = _mm_nt(jnp.where(lo, zero, ar), jnp.concatenate([k_t, b_t], axis=0))
        a_ab = jnp.where(strict, jnp.where(lo, r1[:c_len], r2[:c_len]), 0.0)
        a_ak = jnp.where(strict, jnp.where(lo, r2[:c_len], r1[:c_len]), 0.0)
        a_rb = jnp.where(incl, jnp.where(lo, r1[c_len:], r2[c_len:]), 0.0)
        a_rk = jnp.where(incl, jnp.where(lo, r2[c_len:], r1[c_len:]), 0.0)
        items.append(dict(
            rows=rows, pl=pl_, pair=p, a_ab=a_ab, a_ak=a_ak.astype(BF16),
            a_r=jnp.concatenate([a_rb, a_rk], axis=1).astype(BF16),
            a_t=a_t, r_t=r_t, bk=bk, v_sw=stack(v, first_lo=False).astype(BF16), v=v,
            gl_row=gam_last[c][:, pl_]))

    ms = [it["a_ab"] for it in items]
    ts = [eye2 + m for m in ms]
    power = 1
    while 2 * power < c_len:
        ms = [_mm(m, stack(m)) for m in ms]
        ts = [t + _mm(t, stack(m)) for t, m in zip(ts, ms)]
        power *= 2

    w_ts = [_mm(t, stack(it["a_t"])).astype(BF16) for t, it in zip(ts, items)]
    akvs = [_mm(it["a_ak"], it["v_sw"]) for it in items]
    u_0s = [_mm(t, stack(akv)) for t, akv in zip(ts, akvs)]

    m_s = [(jnp.where(block_diag, _mm_tn(w_t, it["bk"][:c_len]), 0.0) * it["gl_row"]).astype(BF16)
           for w_t, it in zip(w_ts, items)]
    n_s = [jnp.where(block_diag, _mm_tn(jnp.concatenate([u_0, it["v"]], axis=0), it["bk"]), 0.0) * it["gl_row"]
           for u_0, it in zip(u_0s, items)]

    states = [s_ref[p] for p in pairs]
    for c in range(n_chunks):
        idxs = [c * pairs_per_step + p for p in pairs]
        sbs = [states[p].astype(BF16) for p in pairs]
        for idx, sb in zip(idxs, sbs):
            s0_ref[idx] = sb
        prods = [_mm(sb, m_s[idx]) for idx, sb in zip(idxs, sbs)]
        states = [states[p] * items[idx]["gl_row"] + prods[p] + n_s[idx] for p, idx in zip(pairs, idxs)]
    for p in pairs:
        s_ref[p] = states[p]

    wr_s = [_mm_nt(jnp.concatenate([w_t, it["r_t"]], axis=0), s0_ref[idx])
            for idx, (it, w_t) in enumerate(zip(items, w_ts))]
    y_s = [wr[c_len:] + _mm(it["a_r"], jnp.concatenate([stack(wr[:c_len] + u_0).astype(BF16), it["v_sw"]], axis=0))
           for it, wr, u_0 in zip(items, wr_s, u_0s)]
    for it, y in zip(items, y_s):
        y_ref[it["rows"], it["pl"]] = y

    inv_n = 1.0 / head_size
    for p in pairs:
        pl_ = lanes_of(p)
        y = y_ref[:, pl_]
        mean = pair_sum(y) * inv_n
        yc = y - mean
        var = pair_sum(yc * yc) * inv_n
        y_n = yc * lax.rsqrt(var + gn_eps) * lnw_ref[:, pl_] + lnb_ref[:, pl_]
        o_ref[0, :, pl_] = ((y_n + bonus_ref[:, pl_]) * gate_ref[0, :, pl_].astype(F32)).astype(o_ref.dtype)


def _rwkv_chunk(r, k, v, a, wl, gate, k_k, k_a, r_k, ln_w, ln_b, *, time_tile=512, pairs_per_step=8):
    bsz, t, d = r.shape
    head_size = r_k.shape[1]
    assert 2 * head_size == LANES and d % (LANES * pairs_per_step) == 0
    time_tile = min(time_tile, t)
    assert t % time_tile == 0
    width = LANES * pairs_per_step
    n_pairs = d // width
    blk = pl.BlockSpec((1, time_tile, width), lambda b, p, i: (b, i, p))
    vec = pl.BlockSpec((1, width), lambda b, p, i: (0, p))
    tri = jnp.asarray(np.tril(np.ones((CHUNK, CHUNK), np.float32)), BF16)
    flat = lambda x: x.reshape(1, d).astype(F32)
    return pl.pallas_call(
        functools.partial(_rwkv_chunk_kernel, n_chunks=time_tile // CHUNK, pairs_per_step=pairs_per_step,
                          head_size=head_size, gn_eps=1e-5 * head_size * head_size),
        grid=(bsz, n_pairs, t // time_tile),
        in_specs=[blk] * 6 + [vec] * 5 + [pl.BlockSpec((CHUNK, CHUNK), lambda b, p, i: (0, 0))],
        out_specs=blk,
        out_shape=jax.ShapeDtypeStruct((bsz, t, d), BF16),
        scratch_shapes=[pltpu.VMEM((pairs_per_step, LANES, LANES), F32)]
        + [pltpu.VMEM((time_tile, width), BF16)] * 4 + [pltpu.VMEM((time_tile, width), F32)] * 2
        + [pltpu.VMEM((time_tile // CHUNK * pairs_per_step, LANES, LANES), BF16)],
        compiler_params=_params("parallel", "parallel", "arbitrary"),
        name="rwkv_chunk",
    )(r, k, v, a, wl, gate, flat(k_k), flat(k_a), flat(r_k), flat(ln_w), flat(ln_b), tri)


def _rwkv7_mixer(h, g, scale, shift, mu, w_rkv, w0, w1, w2, a0, a1, a2, g1, g2, k_k, k_a, r_k, ln_w, ln_b):
    r, k, v, a, wl, gt = _rwkv_proj(h, g, scale, shift, mu, w_rkv, w0, w1, w2, a0, a1, a2, g1, g2)
    return _rwkv_chunk(r, k, v, a, wl, gt, k_k, k_a, r_k, ln_w, ln_b)


def kernel(x, c, norm_g, ada_w, ada_b, a_w_in, a_conv, a_A_log, a_dt_bias, a_norm, a_w_out, b_w_in, hgrn_lb_logits, b_norm, b_w_out, c_mu, c_w_rkv, c_w0, c_w1, c_w2, c_a0, c_a1, c_a2, c_g1, c_g2, c_k_k, c_k_a, c_r_k, c_ln_w, c_ln_b, c_w_out, f_w_up, f_conv_w, f_conv_b, f_w_down, final_g):
    depth = ada_w.shape[0]
    n_mixers = 3
    mod = _ada_mod(c, ada_w, ada_b)
    w_up_all = f_w_up.astype(BF16)
    w_down_all = f_w_down.astype(BF16)
    h = x
    for i in range(depth):
        shift1, scale1, gate1, shift2, scale2, gate2 = (mod[i, :, s] for s in range(6))
        kind, j = i % n_mixers, i // n_mixers
        if kind == 0:
            y = _deltanet_mixer(h, norm_g[i, 0], scale1, shift1, a_w_in[j], a_conv[j], a_A_log[j], a_dt_bias[j],
                                a_norm[j])
            w_out = a_w_out[j]
        elif kind == 1:
            y = _hgrn2_mixer(h, norm_g[i, 0], scale1, shift1, b_w_in[j], hgrn_lb_logits, i, b_norm[j])
            w_out = b_w_out[j]
        else:
            y = _rwkv7_mixer(h, norm_g[i, 0], scale1, shift1, c_mu[j], c_w_rkv[j], c_w0[j], c_w1[j], c_w2[j],
                             c_a0[j], c_a1[j], c_a2[j], c_g1[j], c_g2[j], c_k_k[j], c_k_a[j], c_r_k[j], c_ln_w[j],
                             c_ln_b[j])
            w_out = c_w_out[j]
        h = _mix_out_ffn(y, w_out, gate1, h, norm_g[i, 1], scale2, shift2, gate2, w_up_all, f_conv_w[i],
                         f_conv_b[i], w_down_all, i, final_g if i == depth - 1 else None)
    return h
```

```python
import functools

import numpy as np
import jax
import jax.numpy as jnp
from jax import lax
from jax.experimental import pallas as pl
from jax.experimental.pallas import tpu as pltpu

F32 = jnp.float32
BF16 = jnp.bfloat16

CHUNK = 64
NORM_EPS = 1e-6
LANES = 128
SUBLANES = 8
VMEM_LIMIT_BYTES = 56 * 1024 * 1024


def _params(*sem):
    return pltpu.CompilerParams(dimension_semantics=sem, vmem_limit_bytes=VMEM_LIMIT_BYTES)


def _mm(a, b):
    return jnp.dot(a.astype(BF16), b.astype(BF16), preferred_element_type=F32)


def _mm_nt(a, b):
    return lax.dot_general(a.astype(BF16), b.astype(BF16), (((1,), (1,)), ((), ())),
                           preferred_element_type=F32)


def _mm_tn(a, b):
    return lax.dot_general(a.astype(BF16), b.astype(BF16), (((0,), (0,)), ((), ())),
                           preferred_element_type=F32)


def _split2(x):
    hi = x.astype(BF16)
    lo = (x - hi.astype(F32)).astype(BF16)
    return hi, lo


NEG_LOG2_E = -1.4426950408889634


def _sigmoid(x):
    return 1.0 / (1.0 + jnp.exp2(x * NEG_LOG2_E))


def _silu(x):
    return x * _sigmoid(x)


def _softplus(x):
    return jnp.maximum(x, 0.0) + jnp.log(1.0 + jnp.exp(-jnp.abs(x)))


def _norm_mod(h, g, scale, shift):
    ms = jnp.mean(h * h, axis=-1, keepdims=True)
    y = h * lax.rsqrt(ms + NORM_EPS)
    return (y * g) * (1.0 + scale) + shift


def _shift_rows(x, prev8, s):
    rows, cols = x.shape
    rot = pltpu.roll(x.reshape(rows // SUBLANES, SUBLANES, cols), s, axis=1)
    prev = jnp.concatenate([pltpu.roll(prev8, s, axis=0)[None], rot[:-1]], axis=0)
    sub = lax.broadcasted_iota(jnp.int32, (1, SUBLANES, cols), 1)
    return jnp.where(sub < s, prev, rot).reshape(rows, cols)


def _mod_kernel(c_ref, w_ref, b_ref, o_ref):
    c = c_ref[...]
    o_ref[0] = _mm(_silu(c), w_ref[0]) + b_ref[0]


def _ada_mod(c, ada_w, ada_b):
    depth, d, d6 = ada_w.shape
    bsz = c.shape[0]
    rows = -(-bsz // SUBLANES) * SUBLANES
    c_pad = jnp.zeros((rows, d), F32).at[:bsz].set(c)
    out = pl.pallas_call(
        _mod_kernel,
        grid=(depth, d6 // d),
        in_specs=[pl.BlockSpec((rows, d), lambda i, j: (0, 0)),
                  pl.BlockSpec((1, d, d), lambda i, j: (i, 0, j)),
                  pl.BlockSpec((1, 1, d), lambda i, j: (i, 0, j))],
        out_specs=pl.BlockSpec((1, rows, d), lambda i, j: (i, 0, j)),
        out_shape=jax.ShapeDtypeStruct((depth, rows, d6), F32),
        compiler_params=_params("parallel", "parallel"),
        name="ada_mod",
    )(c_pad, ada_w, ada_b.reshape(depth, 1, d6))
    return out[:, :bsz].reshape(depth, bsz, d6 // d, d)


def _proj_kernel(h_ref, g_ref, sc_ref, sh_ref, w_ref, o_ref, *, n_col_tiles, col_tile):
    ub = _norm_mod(h_ref[0], g_ref[...], sc_ref[0], sh_ref[0]).astype(BF16)
    for j in range(n_col_tiles):
        sl = slice(j * col_tile, (j + 1) * col_tile)
        o_ref[0, :, sl] = jnp.dot(ub, w_ref[:, sl], preferred_element_type=F32).astype(o_ref.dtype)


def _proj(h, g, scale, shift, w, *, row_tile=512, col_tile=1024, out_dtype=BF16):
    bsz, t, d = h.shape
    n = w.shape[1]
    col_tile = min(col_tile, n)
    row_tile = min(row_tile, t)
    assert t % row_tile == 0 and n % col_tile == 0
    return pl.pallas_call(
        functools.partial(_proj_kernel, n_col_tiles=n // col_tile, col_tile=col_tile),
        grid=(bsz, t // row_tile),
        in_specs=[pl.BlockSpec((1, row_tile, d), lambda b, i: (b, i, 0)),
                  pl.BlockSpec((1, d), lambda b, i: (0, 0)),
                  pl.BlockSpec((1, 1, d), lambda b, i: (b, 0, 0)),
                  pl.BlockSpec((1, 1, d), lambda b, i: (b, 0, 0)),
                  pl.BlockSpec((d, n), lambda b, i: (0, 0), pipeline_mode=pl.Buffered(1))],
        out_specs=pl.BlockSpec((1, row_tile, n), lambda b, i: (b, i, 0)),
        out_shape=jax.ShapeDtypeStruct((bsz, t, n), out_dtype),
        compiler_params=_params("parallel", "parallel"),
        name="norm_proj",
    )(h, g.reshape(1, d), scale.reshape(bsz, 1, d), shift.reshape(bsz, 1, d), w)


def _dn_proj_kernel(h_ref, g_ref, sc_ref, sh_ref, w_ref, wst_ref, cw_ref, o_ref, ost_ref, carry_ref, *,
                    n_conv_cols, head_dim, col_tile):
    @pl.when(pl.program_id(1) == 0)
    def _():
        carry_ref[...] = jnp.zeros_like(carry_ref)

    u = _norm_mod(h_ref[0], g_ref[...], sc_ref[0], sh_ref[0])
    ub = u.astype(BF16)
    n_total = o_ref.shape[2]
    key_cols = n_conv_cols // 3
    n_tiles = n_total // col_tile
    tile_dot = lambda j: jnp.dot(ub, w_ref[:, j * col_tile:(j + 1) * col_tile], preferred_element_type=F32)
    x_next = tile_dot(0)
    for j in range(n_tiles):
        sl = slice(j * col_tile, (j + 1) * col_tile)
        x = x_next
        if j + 1 < n_tiles:
            x_next = tile_dot(j + 1)
        if j * col_tile < n_conv_cols:
            prev8 = carry_ref[:, sl]
            cw = cw_ref[:, sl]
            y = _silu(_shift_rows(x, prev8, 3) * cw[0:1] + _shift_rows(x, prev8, 2) * cw[1:2]
                      + _shift_rows(x, prev8, 1) * cw[2:3] + x * cw[3:4])
            carry_ref[:, sl] = x[-SUBLANES:]
            if j * col_tile < 2 * key_cols:
                post = head_dim ** -0.5 if j * col_tile < key_cols else 1.0
                for hd in range(col_tile // head_dim):
                    hl = slice(hd * head_dim, (hd + 1) * head_dim)
                    yh = y[:, hl]
                    inv = lax.rsqrt(jnp.sum(yh * yh, axis=-1, keepdims=True) + 1e-6) * post
                    o_ref[0, :, j * col_tile + hd * head_dim:j * col_tile + (hd + 1) * head_dim] = (
                        yh * inv).astype(o_ref.dtype)
            else:
                o_ref[0, :, sl] = y.astype(o_ref.dtype)
        else:
            o_ref[0, :, sl] = x.astype(o_ref.dtype)
    u_hi, u_lo = _split2(u)
    w_hi, w_lo = _split2(wst_ref[...])
    ost_ref[0] = _mm_nt(w_hi, u_hi) + _mm_nt(w_hi, u_lo) + _mm_nt(w_lo, u_hi)


def _dn_proj(h, g, scale, shift, w, w_small_t, conv_w, head_dim, *, row_tile=512, col_tile=1024):
    bsz, t, d = h.shape
    n = w.shape[1]
    n_conv = conv_w.shape[1]
    row_tile = min(row_tile, t)
    r = w_small_t.shape[0]
    assert t % row_tile == 0 and n % col_tile == 0 and n_conv % col_tile == 0 and (n_conv // 3) % col_tile == 0
    const = lambda b, i: (0, 0)
    per_b = lambda b, i: (b, 0, 0)
    return pl.pallas_call(
        functools.partial(_dn_proj_kernel, n_conv_cols=n_conv, head_dim=head_dim, col_tile=col_tile),
        grid=(bsz, t // row_tile),
        in_specs=[pl.BlockSpec((1, row_tile, d), lambda b, i: (b, i, 0)),
                  pl.BlockSpec((1, d), const),
                  pl.BlockSpec((1, 1, d), per_b),
                  pl.BlockSpec((1, 1, d), per_b),
                  pl.BlockSpec((d, n), const, pipeline_mode=pl.Buffered(1)),
                  pl.BlockSpec((r, d), const),
                  pl.BlockSpec(conv_w.shape, const)],
        out_specs=[pl.BlockSpec((1, row_tile, n), lambda b, i: (b, i, 0)),
                   pl.BlockSpec((1, r, row_tile), lambda b, i: (b, 0, i))],
        out_shape=[jax.ShapeDtypeStruct((bsz, t, n), BF16), jax.ShapeDtypeStruct((bsz, r, t), F32)],
        scratch_shapes=[pltpu.VMEM((SUBLANES, n_conv), F32)],
        compiler_params=_params("parallel", "arbitrary"),
        name="deltanet_proj",
    )(h, g.reshape(1, d), scale.reshape(bsz, 1, d), shift.reshape(bsz, 1, d), w, w_small_t, conv_w)


def _ffn_kernel(x_ref, wo_ref, gate1_ref, h_ref, g_ref, sc_ref, sh_ref, gate_ref, wv_ref, wg_ref, cw_ref, cb_ref,
                wd_ref, fg_ref, o_ref, act_ref, carry_ref, *, d_ff, col_tile, final_norm):
    i = pl.program_id(1)
    h = h_ref[0] + gate1_ref[0] * jnp.dot(x_ref[0], wo_ref[...], preferred_element_type=F32)
    ub = _norm_mod(h, g_ref[...], sc_ref[0], sh_ref[0]).astype(BF16)

    @pl.when(i == 0)
    def _():
        carry_ref[...] = jnp.zeros_like(carry_ref)

    def conv(up, prev8, w3, bias):
        return (_shift_rows(up, prev8, 2) * w3[0:1] + _shift_rows(up, prev8, 1) * w3[1:2]
                + up * w3[2:3] + bias)

    n_tiles = d_ff // col_tile

    def up_proj(j):
        sl = slice(j * col_tile, (j + 1) * col_tile)
        return (jnp.dot(ub, wv_ref[:, sl], preferred_element_type=F32),
                jnp.dot(ub, wg_ref[:, sl], preferred_element_type=F32))

    nxt = up_proj(0)
    for j in range(n_tiles):
        sl = slice(j * col_tile, (j + 1) * col_tile)
        sl_g = slice(d_ff + j * col_tile, d_ff + (j + 1) * col_tile)
        up_v, up_g = nxt
        if j + 1 < n_tiles:
            nxt = up_proj(j + 1)
        val = conv(up_v, carry_ref[:, sl], cw_ref[:, sl], cb_ref[:, sl])
        gat = conv(up_g, carry_ref[:, sl_g], cw_ref[:, sl_g], cb_ref[:, sl_g])
        carry_ref[:, sl] = up_v[-SUBLANES:]
        carry_ref[:, sl_g] = up_g[-SUBLANES:]
        act_ref[:, sl] = (val * _silu(gat)).astype(BF16)

    y = jnp.dot(act_ref[...], wd_ref[...], preferred_element_type=F32)
    out = h + gate_ref[0] * y
    if final_norm:
        ms = jnp.mean(out * out, axis=-1, keepdims=True)
        out = out * lax.rsqrt(ms + NORM_EPS) * fg_ref[...]
    o_ref[0] = out


def _mix_out_ffn(x, w_out, gate1, h, g, scale, shift, gate, w_up_all, conv_w, conv_b, w_down_all, layer,
                 final_g=None, *, row_tile=1024, col_tile=256):
    bsz, t, d = h.shape
    k_in = x.shape[2]
    d_ff = w_down_all.shape[1]
    row_tile = min(row_tile, t)
    assert t % row_tile == 0 and d_ff % col_tile == 0
    final_norm = final_g is not None
    fg = (final_g if final_norm else jnp.ones((d,), F32)).reshape(1, d)
    const = lambda b, i: (0, 0)
    per_b = lambda b, i: (b, 0, 0)
    tile = lambda width: pl.BlockSpec((1, row_tile, width), lambda b, i: (b, i, 0))
    resident = lambda shape: pl.BlockSpec(shape, const, pipeline_mode=pl.Buffered(1))
    layer_block = lambda shape, col: pl.BlockSpec((None,) + shape, lambda b, i: (layer, 0, col),
                                                  pipeline_mode=pl.Buffered(1))
    vec = lambda: pl.BlockSpec((1, 1, d), per_b)
    return pl.pallas_call(
        functools.partial(_ffn_kernel, d_ff=d_ff, col_tile=col_tile, final_norm=final_norm),
        grid=(bsz, t // row_tile),
        in_specs=[tile(k_in), resident((k_in, d)), vec(),
                  tile(d), pl.BlockSpec((1, d), const), vec(), vec(), vec(),
                  layer_block((d, d_ff), 0), layer_block((d, d_ff), 1),
                  pl.BlockSpec((3, 2 * d_ff), const),
                  pl.BlockSpec((1, 2 * d_ff), const),
                  layer_block((d_ff, d), 0),
                  pl.BlockSpec((1, d), const)],
        out_specs=tile(d),
        out_shape=jax.ShapeDtypeStruct((bsz, t, d), F32),
        scratch_shapes=[pltpu.VMEM((row_tile, d_ff), BF16),
                        pltpu.VMEM((SUBLANES, 2 * d_ff), F32)],
        compiler_params=_params("parallel", "arbitrary"),
        name="mix_out_conv_glu_ffn",
    )(x, w_out.astype(BF16), gate1.reshape(bsz, 1, d),
      h, g.reshape(1, d), scale.reshape(bsz, 1, d), shift.reshape(bsz, 1, d), gate.reshape(bsz, 1, d),
      w_up_all, w_up_all, conv_w, conv_b.reshape(1, 2 * d_ff), w_down_all, fg)


def _head_rms_gate(o, norm_w, z):
    ms = jnp.mean(o * o, axis=-1, keepdims=True)
    return (o * lax.rsqrt(ms + NORM_EPS) * norm_w) * _silu(z)


def _dn_kernel(q_ref, k_ref, v_ref, z_ref, abt_ref, alog_ref, dtb_ref, nw_ref,
               o_ref, s_ref, s0_ref, *, n_chunks, n_heads, heads_per_step):
    head0 = pl.program_id(1) * heads_per_step

    @pl.when(pl.program_id(2) == 0)
    def _():
        s_ref[...] = jnp.zeros_like(s_ref)

    ri = lax.broadcasted_iota(jnp.int32, (CHUNK, CHUNK), 0)
    ci = lax.broadcasted_iota(jnp.int32, (CHUNK, CHUNK), 1)
    nw = nw_ref[...]

    gates = []
    for j in range(heads_per_step):
        a_row = abt_ref[0, pl.ds(head0 + j, 1), :]
        b_row = abt_ref[0, pl.ds(n_heads + head0 + j, 1), :]
        g_all = -jnp.exp(alog_ref[j][:, :1]) * _softplus(a_row + dtb_ref[j][:, :1])
        gates.append((g_all, _sigmoid(b_row)))

    chunks = []
    for c, j in [(c, j) for c in range(n_chunks) for j in range(heads_per_step)]:
        rows = pl.ds(c * CHUNK, CHUNK)
        lanes = slice(c * CHUNK, (c + 1) * CHUNK)
        hl = slice(j * LANES, (j + 1) * LANES)
        g_all, beta_all = gates[j]
        q = q_ref[0, rows, hl].astype(F32)
        k = k_ref[0, rows, hl].astype(F32)
        v = v_ref[0, rows, hl].astype(F32)
        g_row = g_all[:, lanes]
        g_col = jnp.sum(jnp.where(ci <= ri, g_row, 0.0), axis=1, keepdims=True)
        g_rowc = jnp.sum(jnp.where(ci == ri, g_col, 0.0), axis=0, keepdims=True)
        beta_col = jnp.sum(jnp.where(ci == ri, beta_all[:, lanes], 0.0), axis=1, keepdims=True)
        g_last = g_col[CHUNK - 1:CHUNK, :]
        decay = jnp.where(ci <= ri, jnp.exp(jnp.minimum(g_col - g_rowc, 0.0)), 0.0)
        e_g = jnp.exp(g_col)
        kb = k * beta_col
        chunks.append(dict(
            rows=rows, hl=hl, head=j, neg_l=jnp.where(ci < ri, -(_mm_nt(kb, k) * decay), 0.0),
            kbe=(kb * e_g).astype(BF16), vb=(v * beta_col).astype(BF16),
            attn=(_mm_nt(q, k) * decay).astype(BF16), qe=(q * e_g).astype(BF16),
            kd=(k * jnp.exp(g_last - g_col)).astype(BF16), gl=jnp.exp(g_last)))

    eye = jnp.where(ri == ci, 1.0, 0.0)
    ms = [ch["neg_l"] for ch in chunks]
    ps = [eye + m for m in ms]
    power = 1
    while 2 * power < CHUNK:
        ms = [_mm(m, m) for m in ms]
        ps = [p + _mm(p, m) for p, m in zip(ps, ms)]
        power *= 2

    ws = [_mm(p, ch["kbe"]).astype(BF16) for p, ch in zip(ps, chunks)]
    us = [_mm(p, ch["vb"]) for p, ch in zip(ps, chunks)]

    a_s = [(-_mm_tn(w, ch["kd"])).astype(BF16) for w, ch in zip(ws, chunks)]
    n_s = [_mm_tn(u, ch["kd"]) for u, ch in zip(us, chunks)]

    heads = range(heads_per_step)
    states = [s_ref[j] for j in heads]
    for c in range(n_chunks):
        idxs = [c * heads_per_step + j for j in heads]
        sbs = [states[j].astype(BF16) for j in heads]
        for idx, sb in zip(idxs, sbs):
            s0_ref[idx] = sb
        prods = [_mm(sb, a_s[idx]) for idx, sb in zip(idxs, sbs)]
        states = [states[j] * chunks[idx]["gl"] + prods[j] + n_s[idx] for j, idx in zip(heads, idxs)]
    for j in heads:
        s_ref[j] = states[j]

    wq_s = [_mm_nt(jnp.concatenate([w, ch["qe"]], axis=0), s0_ref[idx])
            for idx, (w, ch) in enumerate(zip(ws, chunks))]
    o_s = [wq[CHUNK:] + _mm(ch["attn"], u - wq[:CHUNK]) for wq, u, ch in zip(wq_s, us, chunks)]
    for ch, o in zip(chunks, o_s):
        z = z_ref[0, ch["rows"], ch["hl"]].astype(F32)
        o_ref[0, ch["rows"], ch["hl"]] = _head_rms_gate(o, nw, z).astype(o_ref.dtype)


def _deltanet_core(proj, abt, a_log, dt_bias, norm_w, *, time_tile=512, heads_per_step=8):
    bsz, t, _ = proj.shape
    n_heads = a_log.shape[0]
    dv = norm_w.shape[0]
    time_tile = min(time_tile, t)
    assert dv == LANES and proj.shape[2] == 4 * n_heads * LANES and t % time_tile == 0
    assert n_heads % heads_per_step == 0
    hh = n_heads // heads_per_step
    width = heads_per_step * LANES
    blk = lambda off: pl.BlockSpec((1, time_tile, width), lambda b, h, i: (b, i, off + h))
    per_head = pl.BlockSpec((heads_per_step, 1, LANES), lambda b, h, i: (h, 0, 0))
    bcast = lambda a: jnp.broadcast_to(a.astype(F32)[:, None, None], (n_heads, 1, LANES))
    return pl.pallas_call(
        functools.partial(_dn_kernel, n_chunks=time_tile // CHUNK, n_heads=n_heads, heads_per_step=heads_per_step),
        grid=(bsz, hh, t // time_tile),
        in_specs=[blk(0), blk(hh), blk(2 * hh), blk(3 * hh),
                  pl.BlockSpec((1, 2 * n_heads, time_tile), lambda b, h, i: (b, 0, i)),
                  per_head, per_head,
                  pl.BlockSpec((1, LANES), lambda b, h, i: (0, 0))],
        out_specs=pl.BlockSpec((1, time_tile, width), lambda b, h, i: (b, i, h)),
        out_shape=jax.ShapeDtypeStruct((bsz, t, n_heads * dv), BF16),
        scratch_shapes=[pltpu.VMEM((heads_per_step, LANES, LANES), F32),
                        pltpu.VMEM((time_tile // CHUNK * heads_per_step, LANES, LANES), BF16)],
        compiler_params=_params("parallel", "parallel", "arbitrary"),
        name="deltanet_chunk",
    )(proj, proj, proj, proj, abt, bcast(a_log), bcast(dt_bias), norm_w.reshape(1, dv))


def _deltanet_mixer(h, g, scale, shift, w_in, conv_w, a_log, dt_bias, norm_w):
    n_heads = a_log.shape[0]
    n_main = w_in.shape[1] - 2 * n_heads
    proj, abt = _dn_proj(h, g, scale, shift, w_in[:, :n_main].astype(BF16), w_in[:, n_main:].T, conv_w,
                         norm_w.shape[0])
    return _deltanet_core(proj, abt, a_log, dt_bias, norm_w)


def _gla_levels():
    levels, s = [], CHUNK // 2
    while s >= 1:
        levels.append(s)
        s //= 2
    return levels


def _gla_sum_matrix():
    c = CHUNK
    mats = [np.tril(np.ones((c, c), np.float32))]
    for s in _gla_levels():
        m_l = np.zeros((c, c), np.float32)
        for i in range(c):
            mid = (i // (2 * s)) * 2 * s + s
            if i >= mid:
                m_l[i, mid:i + 1] = 1.0
            else:
                m_l[i, i + 1:mid] = 1.0
        mats.append(m_l)
    return np.concatenate(mats, axis=0)


def _gla_kernel(q_ref, f_ref, i_ref, g_ref, lbl_ref, nw_ref, sm_ref, o_ref, st_ref, *, n_chunks, heads_per_step,
                layer):
    @pl.when(pl.program_id(2) == 0)
    def _():
        st_ref[...] = jnp.zeros_like(st_ref)

    logits = lbl_ref[...]
    e = jnp.exp(logits - jnp.max(logits, axis=0, keepdims=True))
    if layer == 0:
        lb = jnp.zeros((1, logits.shape[1]), F32)
    else:
        lb = jnp.sum(e[1:layer + 1], axis=0, keepdims=True) / jnp.sum(e, axis=0, keepdims=True)
    nw = nw_ref[...]
    sm = sm_ref[...]
    ri = lax.broadcasted_iota(jnp.int32, (CHUNK, CHUNK), 0)
    ci = lax.broadcasted_iota(jnp.int32, (CHUNK, CHUNK), 1)
    row = lax.broadcasted_iota(jnp.int32, (CHUNK, 1), 0)
    levels = _gla_levels()
    heads = range(heads_per_step)
    chunk_rows = lambda c: slice(c * CHUNK, (c + 1) * CHUNK)
    head_lanes = lambda j: slice(j * LANES, (j + 1) * LANES)

    q_all = _silu(q_ref[0].astype(F32))
    f_all = lb + (1.0 - lb) * _sigmoid(f_ref[0].astype(F32))
    k_all = 1.0 - f_all
    lf_hi, lf_lo = _split2(jnp.log(f_all))

    sums = [jnp.dot(sm, lf_hi[chunk_rows(c)], preferred_element_type=F32)
            + jnp.dot(sm, lf_lo[chunk_rows(c)], preferred_element_type=F32) for c in range(n_chunks)]

    scaled = []
    for c in range(n_chunks):
        q, k, sc = q_all[chunk_rows(c)], k_all[chunk_rows(c)], sums[c]
        b = sc[:CHUNK]
        b_last = b[CHUNK - 1:CHUNK, :]
        lv = []
        for l, s in enumerate(levels):
            ex = jnp.exp(sc[(l + 1) * CHUNK:(l + 2) * CHUNK])
            upper = (row & (2 * s - 1)) >= s
            lv.append((jnp.where(upper, q * ex, 0.0).astype(BF16), jnp.where(upper, 0.0, k * ex).astype(BF16)))
        scaled.append(dict(q=q.astype(BF16), k=k.astype(BF16), lv=lv, qb=(q * jnp.exp(b)).astype(BF16),
                           kd=(k * jnp.exp(b_last - b)).astype(BF16), f_last=jnp.exp(b_last),
                           v=i_ref[0, chunk_rows(c), :].astype(BF16)))
    items = [(c, j) for c in range(n_chunks) for j in heads]

    a_s = [jnp.where(ri == ci, _mm_nt(scaled[c]["q"][:, head_lanes(j)], scaled[c]["k"][:, head_lanes(j)]), 0.0)
           for c, j in items]
    for l, s in enumerate(levels):
        same = (ri // (2 * s)) == (ci // (2 * s))
        parts = [_mm_nt(scaled[c]["lv"][l][0][:, head_lanes(j)], scaled[c]["lv"][l][1][:, head_lanes(j)])
                 for c, j in items]
        a_s = [a + jnp.where(same, part, 0.0) for a, part in zip(a_s, parts)]

    kv_s = [_mm_tn(scaled[c]["v"][:, head_lanes(j)], scaled[c]["kd"][:, head_lanes(j)]) for c, j in items]
    states = [st_ref[j] for j in heads]
    starts = []
    for idx, (c, j) in enumerate(items):
        starts.append(states[j].astype(BF16))
        states[j] = states[j] * scaled[c]["f_last"][:, head_lanes(j)] + kv_s[idx]
    for j in heads:
        st_ref[j] = states[j]

    inter = [_mm_nt(scaled[c]["qb"][:, head_lanes(j)], st0) for (c, j), st0 in zip(items, starts)]
    o_s = [x + _mm(a, scaled[c]["v"][:, head_lanes(j)]) for x, a, (c, j) in zip(inter, a_s, items)]
    for (c, j), o in zip(items, o_s):
        g = g_ref[0, chunk_rows(c), head_lanes(j)].astype(F32)
        o_ref[0, chunk_rows(c), head_lanes(j)] = _head_rms_gate(o, nw, g).astype(o_ref.dtype)


def _gla_core(proj, lb_logits, layer, norm_w, *, time_tile=512, heads_per_step=4):
    bsz, t, n = proj.shape
    dv = norm_w.shape[0]
    n_heads = n // (4 * LANES)
    time_tile = min(time_tile, t)
    assert dv == LANES and n == 4 * n_heads * LANES and t % time_tile == 0 and n_heads % heads_per_step == 0
    hh = n_heads // heads_per_step
    width = heads_per_step * LANES
    blk = lambda off: pl.BlockSpec((1, time_tile, width), lambda b, h, i: (b, i, off + h))
    sm = jnp.asarray(_gla_sum_matrix(), BF16)
    return pl.pallas_call(
        functools.partial(_gla_kernel, n_chunks=time_tile // CHUNK, heads_per_step=heads_per_step, layer=layer),
        grid=(bsz, hh, t // time_tile),
        in_specs=[blk(0), blk(hh), blk(2 * hh), blk(3 * hh),
                  pl.BlockSpec((lb_logits.shape[0], width), lambda b, h, i: (0, h)),
                  pl.BlockSpec((1, LANES), lambda b, h, i: (0, 0)),
                  pl.BlockSpec(sm.shape, lambda b, h, i: (0, 0))],
        out_specs=pl.BlockSpec((1, time_tile, width), lambda b, h, i: (b, i, h)),
        out_shape=jax.ShapeDtypeStruct((bsz, t, n_heads * dv), BF16),
        scratch_shapes=[pltpu.VMEM((heads_per_step, LANES, LANES), F32)],
        compiler_params=_params("parallel", "parallel", "arbitrary"),
        name="gla_chunk",
    )(proj, proj, proj, proj, lb_logits.astype(F32), norm_w.reshape(1, dv), sm)


def _hgrn2_mixer(h, g, scale, shift, w_in, lb_logits, layer, norm_w):
    proj = _proj(h, g, scale, shift, w_in.astype(BF16))
    return _gla_core(proj, lb_logits, layer, norm_w)


def _rwkv_proj_kernel(h_ref, g_ref, sc_ref, sh_ref, mu_ref, wrkv_ref, w0_ref, w1_ref, w2_ref, a0_ref, a1_ref,
                      a2_ref, g1_ref, g2_ref, r_ref, k_ref, v_ref, a_ref, wl_ref, gate_ref, carry_ref):
    @pl.when(pl.program_id(1) == 0)
    def _():
        carry_ref[...] = jnp.zeros_like(carry_ref)

    u = _norm_mod(h_ref[0], g_ref[...], sc_ref[0], sh_ref[0])
    dx = _shift_rows(u, carry_ref[...], 1) - u
    carry_ref[...] = u[-SUBLANES:]
    mu = mu_ref[...]
    mix = lambda s: (u + dx * mu[s:s + 1]).astype(BF16)
    dot = lambda a, b: jnp.dot(a, b, preferred_element_type=F32)
    r_ref[0] = dot(mix(0), wrkv_ref[0]).astype(r_ref.dtype)
    k_ref[0] = dot(mix(2), wrkv_ref[1]).astype(k_ref.dtype)
    v_ref[0] = dot(mix(3), wrkv_ref[2]).astype(v_ref.dtype)
    w = w0_ref[...] + dot(jnp.tanh(dot(mix(1), w1_ref[...])).astype(BF16), w2_ref[...])
    w = -_softplus(-w) - 0.5
    wl_ref[0] = -jnp.exp(w)
    a_ref[0] = _sigmoid(a0_ref[...] + dot(dot(mix(4), a1_ref[...]).astype(BF16), a2_ref[...]))
    gate_ref[0] = dot(_sigmoid(dot(mix(5), g1_ref[...])).astype(BF16), g2_ref[...]).astype(gate_ref.dtype)


def _pad_lora(w_in, w_out):
    rank = w_in.shape[1]
    pad = -(-rank // LANES) * LANES - rank
    return (jnp.pad(w_in, ((0, 0), (0, pad))).astype(BF16), jnp.pad(w_out, ((0, pad), (0, 0))).astype(BF16))


def _rwkv_proj(h, g, scale, shift, mu, w_rkv, w0, w1, w2, a0, a1, a2, g1, g2, *, row_tile=512):
    bsz, t, d = h.shape
    row_tile = min(row_tile, t)
    assert t % row_tile == 0
    w1p, w2p = _pad_lora(w1, w2)
    a1p, a2p = _pad_lora(a1, a2)
    g1p, g2p = _pad_lora(g1, g2)
    const2 = lambda b, i: (0, 0)
    per_b = lambda b, i: (b, 0, 0)
    tile = pl.BlockSpec((1, row_tile, d), lambda b, i: (b, i, 0))
    full = lambda a: pl.BlockSpec(a.shape, const2)
    vec = pl.BlockSpec((1, d), const2)
    f32_out = jax.ShapeDtypeStruct((bsz, t, d), F32)
    bf16_out = jax.ShapeDtypeStruct((bsz, t, d), BF16)
    return pl.pallas_call(
        _rwkv_proj_kernel,
        grid=(bsz, t // row_tile),
        in_specs=[tile, vec, pl.BlockSpec((1, 1, d), per_b), pl.BlockSpec((1, 1, d), per_b),
                  full(mu), pl.BlockSpec(w_rkv.shape, lambda b, i: (0, 0, 0)),
                  vec, full(w1p), full(w2p), vec, full(a1p), full(a2p), full(g1p), full(g2p)],
        out_specs=[tile] * 6,
        out_shape=[bf16_out] * 3 + [f32_out] * 2 + [bf16_out],
        scratch_shapes=[pltpu.VMEM((SUBLANES, d), F32)],
        compiler_params=_params("parallel", "arbitrary"),
        name="rwkv_proj",
    )(h, g.reshape(1, d), scale.reshape(bsz, 1, d), shift.reshape(bsz, 1, d), mu, w_rkv.astype(BF16),
      w0.reshape(1, d), w1p, w2p, a0.reshape(1, d), a1p, a2p, g1p, g2p)


def _rwkv_chunk_kernel(r_ref, k_ref, v_ref, a_ref, wl_ref, gate_ref, kk_ref, ka_ref, rk_ref, lnw_ref, lnb_ref,
                       tri_ref, o_ref, s_ref, at_ref, rt_ref, bt_ref, kt_ref, bonus_ref, y_ref, s0_ref,
                       *, n_chunks, pairs_per_step, head_size, gn_eps):
    @pl.when(pl.program_id(2) == 0)
    def _():
        s_ref[...] = jnp.zeros_like(s_ref)

    c_len = CHUNK
    lane = lax.broadcasted_iota(jnp.int32, (1, LANES), 1)
    lo = lane < head_size
    lane_t = lax.broadcasted_iota(jnp.int32, (c_len, LANES), 1) & (head_size - 1)
    row_t = lax.broadcasted_iota(jnp.int32, (c_len, LANES), 0)
    strict = lane_t < row_t
    incl = lane_t <= row_t
    eye2 = jnp.where(lane_t == row_t, 1.0, 0.0)
    bd_r = lax.broadcasted_iota(jnp.int32, (LANES, LANES), 0) < head_size
    bd_c = lax.broadcasted_iota(jnp.int32, (LANES, LANES), 1) < head_size
    block_diag = bd_r == bd_c
    tri = tri_ref[...]

    def pair_sum(x):
        s_lo = jnp.sum(jnp.where(lo, x, 0.0), axis=-1, keepdims=True)
        s_hi = jnp.sum(jnp.where(lo, 0.0, x), axis=-1, keepdims=True)
        return jnp.where(lo, s_lo, s_hi)

    def stack(x, first_lo=True):
        x_lo, x_hi = jnp.where(lo, x, 0.0), jnp.where(lo, 0.0, x)
        return jnp.concatenate([x_lo, x_hi] if first_lo else [x_hi, x_lo], axis=0)

    pairs = range(pairs_per_step)
    lanes_of = lambda p: slice(p * LANES, (p + 1) * LANES)
    kk_all, k_all = [], []
    for p in pairs:
        pl_ = lanes_of(p)
        k_raw = k_ref[0, :, pl_]
        a = a_ref[0, :, pl_]
        kx = k_raw * kk_ref[:, pl_]
        kk_all.append(kx * lax.rsqrt(pair_sum(kx * kx) + 1e-6))
        k_all.append(k_raw * (1.0 + (a - 1.0) * ka_ref[:, pl_]))
        bonus_ref[:, pl_] = pair_sum(r_ref[0, :, pl_] * k_all[p] * rk_ref[:, pl_]) * v_ref[0, :, pl_]

    cums = []
    for c in range(n_chunks):
        wl_hi, wl_lo = _split2(wl_ref[0, pl.ds(c * c_len, c_len), :])
        cums.append(jnp.dot(tri, wl_hi, preferred_element_type=F32)
                    + jnp.dot(tri, wl_lo, preferred_element_type=F32))
    cum_all = jnp.concatenate(cums, axis=0)
    gam_last = [jnp.exp(cm[c_len - 1:c_len, :]) for cm in cums]

    for p in pairs:
        pl_ = lanes_of(p)
        cum = cum_all[:, pl_]
        inv_gam = jnp.exp(-cum)
        at_ref[:, pl_] = (-kk_all[p] * jnp.exp(cum - wl_ref[0, :, pl_])).astype(BF16)
        rt_ref[:, pl_] = (r_ref[0, :, pl_] * jnp.exp(cum)).astype(BF16)
        bt_ref[:, pl_] = (kk_all[p] * a_ref[0, :, pl_] * inv_gam).astype(BF16)
        kt_ref[:, pl_] = (k_all[p] * inv_gam).astype(BF16)

    items = []
    for c, p in [(c, p) for c in range(n_chunks) for p in pairs]:
        rows = pl.ds(c * c_len, c_len)
        pl_ = lanes_of(p)
        a_t = at_ref[rows, pl_]
        r_t = rt_ref[rows, pl_]
        b_t = bt_ref[rows, pl_]
        k_t = kt_ref[rows, pl_]
        v = v_ref[0, rows, pl_]
        bk = jnp.concatenate([b_t, k_t], axis=0)
        ar = jnp.concatenate([a_t, r_t], axis=0)
        zero = jnp.zeros_like(ar)
        r1 = _mm_nt(jnp.where(lo, ar, zero), bk)
        r2 = _mm_nt(jnp.where(lo, zero, ar), jnp.concatenate([k_t, b_t], axis=0))
        a_ab = jnp.where(strict, jnp.where(lo, r1[:c_len], r2[:c_len]), 0.0)
        a_ak = jnp.where(strict, jnp.where(lo, r2[:c_len], r1[:c_len]), 0.0)
        a_rb = jnp.where(incl, jnp.where(lo, r1[c_len:], r2[c_len:]), 0.0)
        a_rk = jnp.where(incl, jnp.where(lo, r2[c_len:], r1[c_len:]), 0.0)
        items.append(dict(
            rows=rows, pl=pl_, pair=p, a_ab=a_ab, a_ak=a_ak.astype(BF16),
            a_r=jnp.concatenate([a_rb, a_rk], axis=1).astype(BF16),
            a_t=a_t, r_t=r_t, bk=bk, v_sw=stack(v, first_lo=False).astype(BF16), v=v,
            gl_row=gam_last[c][:, pl_]))

    ms = [it["a_ab"] for it in items]
    ts = [eye2 + m for m in ms]
    power = 1
    while 2 * power < c_len:
        ms = [_mm(m, stack(m)) for m in ms]
        ts = [t + _mm(t, stack(m)) for t, m in zip(ts, ms)]
        power *= 2

    w_ts = [_mm(t, stack(it["a_t"])).astype(BF16) for t, it in zip(ts, items)]
    akvs = [_mm(it["a_ak"], it["v_sw"]) for it in items]
    u_0s = [_mm(t, stack(akv)) for t, akv in zip(ts, akvs)]

    m_s = [(jnp.where(block_diag, _mm_tn(w_t, it["bk"][:c_len]), 0.0) * it["gl_row"]).astype(BF16)
           for w_t, it in zip(w_ts, items)]
    n_s = [jnp.where(block_diag, _mm_tn(jnp.concatenate([u_0, it["v"]], axis=0), it["bk"]), 0.0) * it["gl_row"]
           for u_0, it in zip(u_0s, items)]

    states = [s_ref[p] for p in pairs]
    for c in range(n_chunks):
        idxs = [c * pairs_per_step + p for p in pairs]
        sbs = [states[p].astype(BF16) for p in pairs]
        for idx, sb in zip(idxs, sbs):
            s0_ref[idx] = sb
        prods = [_mm(sb, m_s[idx]) for idx, sb in zip(idxs, sbs)]
        states = [states[p] * items[idx]["gl_row"] + prods[p] + n_s[idx] for p, idx in zip(pairs, idxs)]
    for p in pairs:
        s_ref[p] = states[p]

    wr_s = [_mm_nt(jnp.concatenate([w_t, it["r_t"]], axis=0), s0_ref[idx])
            for idx, (it, w_t) in enumerate(zip(items, w_ts))]
    y_s = [wr[c_len:] + _mm(it["a_r"], jnp.concatenate([stack(wr[:c_len] + u_0).astype(BF16), it["v_sw"]], axis=0))
           for it, wr, u_0 in zip(items, wr_s, u_0s)]
    for it, y in zip(items, y_s):
        y_ref[it["rows"], it["pl"]] = y

    inv_n = 1.0 / head_size
    for p in pairs:
        pl_ = lanes_of(p)
        y = y_ref[:, pl_]
        mean = pair_sum(y) * inv_n
        yc = y - mean
        var = pair_sum(yc * yc) * inv_n
        y_n = yc * lax.rsqrt(var + gn_eps) * lnw_ref[:, pl_] + lnb_ref[:, pl_]
        o_ref[0, :, pl_] = ((y_n + bonus_ref[:, pl_]) * gate_ref[0, :, pl_].astype(F32)).astype(o_ref.dtype)


def _rwkv_chunk(r, k, v, a, wl, gate, k_k, k_a, r_k, ln_w, ln_b, *, time_tile=512, pairs_per_step=8):
    bsz, t, d = r.shape
    head_size = r_k.shape[1]
    assert 2 * head_size == LANES and d % (LANES * pairs_per_step) == 0
    time_tile = min(time_tile, t)
    assert t % time_tile == 0
    width = LANES * pairs_per_step
    n_pairs = d // width
    blk = pl.BlockSpec((1, time_tile, width), lambda b, p, i: (b, i, p))
    vec = pl.BlockSpec((1, width), lambda b, p, i: (0, p))
    tri = jnp.asarray(np.tril(np.ones((CHUNK, CHUNK), np.float32)), BF16)
    flat = lambda x: x.reshape(1, d).astype(F32)
    return pl.pallas_call(
        functools.partial(_rwkv_chunk_kernel, n_chunks=time_tile // CHUNK, pairs_per_step=pairs_per_step,
                          head_size=head_size, gn_eps=1e-5 * head_size * head_size),
        grid=(bsz, n_pairs, t // time_tile),
        in_specs=[blk] * 6 + [vec] * 5 + [pl.BlockSpec((CHUNK, CHUNK), lambda b, p, i: (0, 0))],
        out_specs=blk,
        out_shape=jax.ShapeDtypeStruct((bsz, t, d), BF16),
        scratch_shapes=[pltpu.VMEM((pairs_per_step, LANES, LANES), F32)]
        + [pltpu.VMEM((time_tile, width), BF16)] * 4 + [pltpu.VMEM((time_tile, width), F32)] * 2
        + [pltpu.VMEM((time_tile // CHUNK * pairs_per_step, LANES, LANES), BF16)],
        compiler_params=_params("parallel", "parallel", "arbitrary"),
        name="rwkv_chunk",
    )(r, k, v, a, wl, gate, flat(k_k), flat(k_a), flat(r_k), flat(ln_w), flat(ln_b), tri)


def _rwkv7_mixer(h, g, scale, shift, mu, w_rkv, w0, w1, w2, a0, a1, a2, g1, g2, k_k, k_a, r_k, ln_w, ln_b):
    r, k, v, a, wl, gt = _rwkv_proj(h, g, scale, shift, mu, w_rkv, w0, w1, w2, a0, a1, a2, g1, g2)
    return _rwkv_chunk(r, k, v, a, wl, gt, k_k, k_a, r_k, ln_w, ln_b)


def kernel(x, c, norm_g, ada_w, ada_b, a_w_in, a_conv, a_A_log, a_dt_bias, a_norm, a_w_out, b_w_in, hgrn_lb_logits, b_norm, b_w_out, c_mu, c_w_rkv, c_w0, c_w1, c_w2, c_a0, c_a1, c_a2, c_g1, c_g2, c_k_k, c_k_a, c_r_k, c_ln_w, c_ln_b, c_w_out, f_w_up, f_conv_w, f_conv_b, f_w_down, final_g):
    depth = ada_w.shape[0]
    n_mixers = 3
    mod = _ada_mod(c, ada_w, ada_b)
    w_up_all = f_w_up.astype(BF16)
    w_down_all = f_w_down.astype(BF16)
    h = x
    for i in range(depth):
        shift1, scale1, gate1, shift2, scale2, gate2 = (mod[i, :, s] for s in range(6))
        kind, j = i % n_mixers, i // n_mixers
        if kind == 0:
            y = _deltanet_mixer(h, norm_g[i, 0], scale1, shift1, a_w_in[j], a_conv[j], a_A_log[j], a_dt_bias[j],
                                a_norm[j])
            w_out = a_w_out[j]
        elif kind == 1:
            y = _hgrn2_mixer(h, norm_g[i, 0], scale1, shift1, b_w_in[j], hgrn_lb_logits, i, b_norm[j])
            w_out = b_w_out[j]
        else:
            y = _rwkv7_mixer(h, norm_g[i, 0], scale1, shift1, c_mu[j], c_w_rkv[j], c_w0[j], c_w1[j], c_w2[j],
                             c_a0[j], c_a1[j], c_a2[j], c_g1[j], c_g2[j], c_k_k[j], c_k_a[j], c_r_k[j], c_ln_w[j],
                             c_ln_b[j])
            w_out = c_w_out[j]
        h = _mix_out_ffn(y, w_out, gate1, h, norm_g[i, 1], scale2, shift2, gate2, w_up_all, f_conv_w[i],
                         f_conv_b[i], w_down_all, i, final_g if i == depth - 1 else None)
    return h
```

```python
import functools

import numpy as np
import jax
import jax.numpy as jnp
from jax import lax
from jax.experimental import pallas as pl
from jax.experimental.pallas import tpu as pltpu

F32 = jnp.float32
BF16 = jnp.bfloat16

CHUNK = 64
NORM_EPS = 1e-6
LANES = 128
SUBLANES = 8
VMEM_LIMIT_BYTES = 56 * 1024 * 1024


def _params(*sem):
    return pltpu.CompilerParams(dimension_semantics=sem, vmem_limit_bytes=VMEM_LIMIT_BYTES)


def _mm(a, b):
    return jnp.dot(a.astype(BF16), b.astype(BF16), preferred_element_type=F32)


def _mm_nt(a, b):
    return lax.dot_general(a.astype(BF16), b.astype(BF16), (((1,), (1,)), ((), ())),
                           preferred_element_type=F32)


def _mm_tn(a, b):
    return lax.dot_general(a.astype(BF16), b.astype(BF16), (((0,), (0,)), ((), ())),
                           preferred_element_type=F32)


def _split2(x):
    hi = x.astype(BF16)
    lo = (x - hi.astype(F32)).astype(BF16)
    return hi, lo


NEG_LOG2_E = -1.4426950408889634


def _sigmoid(x):
    return 1.0 / (1.0 + jnp.exp2(x * NEG_LOG2_E))


def _silu(x):
    return x * _sigmoid(x)


def _softplus(x):
    return jnp.maximum(x, 0.0) + jnp.log(1.0 + jnp.exp(-jnp.abs(x)))


def _norm_mod(h, g, scale, shift):
    ms = jnp.mean(h * h, axis=-1, keepdims=True)
    y = h * lax.rsqrt(ms + NORM_EPS)
    return (y * g) * (1.0 + scale) + shift


def _shift_rows(x, prev8, s):
    rows, cols = x.shape
    rot = pltpu.roll(x.reshape(rows // SUBLANES, SUBLANES, cols), s, axis=1)
    prev = jnp.concatenate([pltpu.roll(prev8, s, axis=0)[None], rot[:-1]], axis=0)
    sub = lax.broadcasted_iota(jnp.int32, (1, SUBLANES, cols), 1)
    return jnp.where(sub < s, prev, rot).reshape(rows, cols)


def _mod_kernel(c_ref, w_ref, b_ref, o_ref):
    c = c_ref[...]
    o_ref[0] = _mm(_silu(c), w_ref[0]) + b_ref[0]


def _ada_mod(c, ada_w, ada_b):
    depth, d, d6 = ada_w.shape
    bsz = c.shape[0]
    rows = -(-bsz // SUBLANES) * SUBLANES
    c_pad = jnp.zeros((rows, d), F32).at[:bsz].set(c)
    out = pl.pallas_call(
        _mod_kernel,
        grid=(depth, d6 // d),
        in_specs=[pl.BlockSpec((rows, d), lambda i, j: (0, 0)),
                  pl.BlockSpec((1, d, d), lambda i, j: (i, 0, j)),
                  pl.BlockSpec((1, 1, d), lambda i, j: (i, 0, j))],
        out_specs=pl.BlockSpec((1, rows, d), lambda i, j: (i, 0, j)),
        out_shape=jax.ShapeDtypeStruct((depth, rows, d6), F32),
        compiler_params=_params("parallel", "parallel"),
        name="ada_mod",
    )(c_pad, ada_w, ada_b.reshape(depth, 1, d6))
    return out[:, :bsz].reshape(depth, bsz, d6 // d, d)


def _proj_kernel(h_ref, g_ref, sc_ref, sh_ref, w_ref, o_ref, *, n_col_tiles, col_tile):
    ub = _norm_mod(h_ref[0], g_ref[...], sc_ref[0], sh_ref[0]).astype(BF16)
    for j in range(n_col_tiles):
        sl = slice(j * col_tile, (j + 1) * col_tile)
        o_ref[0, :, sl] = jnp.dot(ub, w_ref[:, sl], preferred_element_type=F32).astype(o_ref.dtype)


def _proj(h, g, scale, shift, w, *, row_tile=512, col_tile=1024, out_dtype=BF16):
    bsz, t, d = h.shape
    n = w.shape[1]
    col_tile = min(col_tile, n)
    row_tile = min(row_tile, t)
    assert t % row_tile == 0 and n % col_tile == 0
    return pl.pallas_call(
        functools.partial(_proj_kernel, n_col_tiles=n // col_tile, col_tile=col_tile),
        grid=(bsz, t // row_tile),
        in_specs=[pl.BlockSpec((1, row_tile, d), lambda b, i: (b, i, 0)),
                  pl.BlockSpec((1, d), lambda b, i: (0, 0)),
                  pl.BlockSpec((1, 1, d), lambda b, i: (b, 0, 0)),
                  pl.BlockSpec((1, 1, d), lambda b, i: (b, 0, 0)),
                  pl.BlockSpec((d, n), lambda b, i: (0, 0), pipeline_mode=pl.Buffered(1))],
        out_specs=pl.BlockSpec((1, row_tile, n), lambda b, i: (b, i, 0)),
        out_shape=jax.ShapeDtypeStruct((bsz, t, n), out_dtype),
        compiler_params=_params("parallel", "parallel"),
        name="norm_proj",
    )(h, g.reshape(1, d), scale.reshape(bsz, 1, d), shift.reshape(bsz, 1, d), w)


def _dn_proj_kernel(h_ref, g_ref, sc_ref, sh_ref, w_ref, wst_ref, cw_ref, o_ref, ost_ref, carry_ref, *,
                    n_conv_cols, head_dim, col_tile):
    @pl.when(pl.program_id(1) == 0)
    def _():
        carry_ref[...] = jnp.zeros_like(carry_ref)

    u = _norm_mod(h_ref[0], g_ref[...], sc_ref[0], sh_ref[0])
    ub = u.astype(BF16)
    n_total = o_ref.shape[2]
    key_cols = n_conv_cols // 3
    for j in range(n_total // col_tile):
        sl = slice(j * col_tile, (j + 1) * col_tile)
        x = jnp.dot(ub, w_ref[:, sl], preferred_element_type=F32)
        if j * col_tile < n_conv_cols:
            prev8 = carry_ref[:, sl]
            cw = cw_ref[:, sl]
            y = _silu(_shift_rows(x, prev8, 3) * cw[0:1] + _shift_rows(x, prev8, 2) * cw[1:2]
                      + _shift_rows(x, prev8, 1) * cw[2:3] + x * cw[3:4])
            carry_ref[:, sl] = x[-SUBLANES:]
            if j * col_tile < 2 * key_cols:
                post = head_dim ** -0.5 if j * col_tile < key_cols else 1.0
                for hd in range(col_tile // head_dim):
                    hl = slice(hd * head_dim, (hd + 1) * head_dim)
                    yh = y[:, hl]
                    inv = lax.rsqrt(jnp.sum(yh * yh, axis=-1, keepdims=True) + 1e-6) * post
                    o_ref[0, :, j * col_tile + hd * head_dim:j * col_tile + (hd + 1) * head_dim] = (
                        yh * inv).astype(o_ref.dtype)
            else:
                o_ref[0, :, sl] = y.astype(o_ref.dtype)
        else:
            o_ref[0, :, sl] = x.astype(o_ref.dtype)
    u_hi, u_lo = _split2(u)
    w_hi, w_lo = _split2(wst_ref[...])
    ost_ref[0] = _mm_nt(w_hi, u_hi) + _mm_nt(w_hi, u_lo) + _mm_nt(w_lo, u_hi)


def _dn_proj(h, g, scale, shift, w, w_small_t, conv_w, head_dim, *, row_tile=512, col_tile=1024):
    bsz, t, d = h.shape
    n = w.shape[1]
    n_conv = conv_w.shape[1]
    row_tile = min(row_tile, t)
    r = w_small_t.shape[0]
    assert t % row_tile == 0 and n % col_tile == 0 and n_conv % col_tile == 0 and (n_conv // 3) % col_tile == 0
    const = lambda b, i: (0, 0)
    per_b = lambda b, i: (b, 0, 0)
    return pl.pallas_call(
        functools.partial(_dn_proj_kernel, n_conv_cols=n_conv, head_dim=head_dim, col_tile=col_tile),
        grid=(bsz, t // row_tile),
        in_specs=[pl.BlockSpec((1, row_tile, d), lambda b, i: (b, i, 0)),
                  pl.BlockSpec((1, d), const),
                  pl.BlockSpec((1, 1, d), per_b),
                  pl.BlockSpec((1, 1, d), per_b),
                  pl.BlockSpec((d, n), const, pipeline_mode=pl.Buffered(1)),
                  pl.BlockSpec((r, d), const),
                  pl.BlockSpec(conv_w.shape, const)],
        out_specs=[pl.BlockSpec((1, row_tile, n), lambda b, i: (b, i, 0)),
                   pl.BlockSpec((1, r, row_tile), lambda b, i: (b, 0, i))],
        out_shape=[jax.ShapeDtypeStruct((bsz, t, n), BF16), jax.ShapeDtypeStruct((bsz, r, t), F32)],
        scratch_shapes=[pltpu.VMEM((SUBLANES, n_conv), F32)],
        compiler_params=_params("parallel", "arbitrary"),
        name="deltanet_proj",
    )(h, g.reshape(1, d), scale.reshape(bsz, 1, d), shift.reshape(bsz, 1, d), w, w_small_t, conv_w)


def _ffn_kernel(x_ref, wo_ref, gate1_ref, h_ref, g_ref, sc_ref, sh_ref, gate_ref, wv_ref, wg_ref, cw_ref, cb_ref,
                wd_ref, fg_ref, o_ref, act_ref, carry_ref, *, d_ff, col_tile, final_norm):
    i = pl.program_id(1)
    h = h_ref[0] + gate1_ref[0] * jnp.dot(x_ref[0], wo_ref[...], preferred_element_type=F32)
    ub = _norm_mod(h, g_ref[...], sc_ref[0], sh_ref[0]).astype(BF16)

    @pl.when(i == 0)
    def _():
        carry_ref[...] = jnp.zeros_like(carry_ref)

    def conv(up, prev8, w3, bias):
        return (_shift_rows(up, prev8, 2) * w3[0:1] + _shift_rows(up, prev8, 1) * w3[1:2]
                + up * w3[2:3] + bias)

    for j in range(d_ff // col_tile):
        sl = slice(j * col_tile, (j + 1) * col_tile)
        sl_g = slice(d_ff + j * col_tile, d_ff + (j + 1) * col_tile)
        up_v = jnp.dot(ub, wv_ref[:, sl], preferred_element_type=F32)
        up_g = jnp.dot(ub, wg_ref[:, sl], preferred_element_type=F32)
        val = conv(up_v, carry_ref[:, sl], cw_ref[:, sl], cb_ref[:, sl])
        gat = conv(up_g, carry_ref[:, sl_g], cw_ref[:, sl_g], cb_ref[:, sl_g])
        carry_ref[:, sl] = up_v[-SUBLANES:]
        carry_ref[:, sl_g] = up_g[-SUBLANES:]
        act_ref[:, sl] = (val * _silu(gat)).astype(BF16)

    y = jnp.dot(act_ref[...], wd_ref[...], preferred_element_type=F32)
    out = h + gate_ref[0] * y
    if final_norm:
        ms = jnp.mean(out * out, axis=-1, keepdims=True)
        out = out * lax.rsqrt(ms + NORM_EPS) * fg_ref[...]
    o_ref[0] = out


def _mix_out_ffn(x, w_out, gate1, h, g, scale, shift, gate, w_up_all, conv_w, conv_b, w_down_all, layer,
                 final_g=None, *, row_tile=1024, col_tile=256):
    bsz, t, d = h.shape
    k_in = x.shape[2]
    d_ff = w_down_all.shape[1]
    row_tile = min(row_tile, t)
    assert t % row_tile == 0 and d_ff % col_tile == 0
    final_norm = final_g is not None
    fg = (final_g if final_norm else jnp.ones((d,), F32)).reshape(1, d)
    const = lambda b, i: (0, 0)
    per_b = lambda b, i: (b, 0, 0)
    tile = lambda width: pl.BlockSpec((1, row_tile, width), lambda b, i: (b, i, 0))
    resident = lambda shape: pl.BlockSpec(shape, const, pipeline_mode=pl.Buffered(1))
    layer_block = lambda shape, col: pl.BlockSpec((None,) + shape, lambda b, i: (layer, 0, col),
                                                  pipeline_mode=pl.Buffered(1))
    vec = lambda: pl.BlockSpec((1, 1, d), per_b)
    return pl.pallas_call(
        functools.partial(_ffn_kernel, d_ff=d_ff, col_tile=col_tile, final_norm=final_norm),
        grid=(bsz, t // row_tile),
        in_specs=[tile(k_in), resident((k_in, d)), vec(),
                  tile(d), pl.BlockSpec((1, d), const), vec(), vec(), vec(),
                  layer_block((d, d_ff), 0), layer_block((d, d_ff), 1),
                  pl.BlockSpec((3, 2 * d_ff), const),
                  pl.BlockSpec((1, 2 * d_ff), const),
                  layer_block((d_ff, d), 0),
                  pl.BlockSpec((1, d), const)],
        out_specs=tile(d),
        out_shape=jax.ShapeDtypeStruct((bsz, t, d), F32),
        scratch_shapes=[pltpu.VMEM((row_tile, d_ff), BF16),
                        pltpu.VMEM((SUBLANES, 2 * d_ff), F32)],
        compiler_params=_params("parallel", "arbitrary"),
        name="mix_out_conv_glu_ffn",
    )(x, w_out.astype(BF16), gate1.reshape(bsz, 1, d),
      h, g.reshape(1, d), scale.reshape(bsz, 1, d), shift.reshape(bsz, 1, d), gate.reshape(bsz, 1, d),
      w_up_all, w_up_all, conv_w, conv_b.reshape(1, 2 * d_ff), w_down_all, fg)


def _head_rms_gate(o, norm_w, z):
    ms = jnp.mean(o * o, axis=-1, keepdims=True)
    return (o * lax.rsqrt(ms + NORM_EPS) * norm_w) * _silu(z)


def _dn_kernel(q_ref, k_ref, v_ref, z_ref, abt_ref, alog_ref, dtb_ref, nw_ref,
               o_ref, s_ref, s0_ref, *, n_chunks, n_heads, heads_per_step):
    head0 = pl.program_id(1) * heads_per_step

    @pl.when(pl.program_id(2) == 0)
    def _():
        s_ref[...] = jnp.zeros_like(s_ref)

    ri = lax.broadcasted_iota(jnp.int32, (CHUNK, CHUNK), 0)
    ci = lax.broadcasted_iota(jnp.int32, (CHUNK, CHUNK), 1)
    nw = nw_ref[...]

    gates = []
    for j in range(heads_per_step):
        a_row = abt_ref[0, pl.ds(head0 + j, 1), :]
        b_row = abt_ref[0, pl.ds(n_heads + head0 + j, 1), :]
        g_all = -jnp.exp(alog_ref[j][:, :1]) * _softplus(a_row + dtb_ref[j][:, :1])
        gates.append((g_all, _sigmoid(b_row)))

    chunks = []
    for c, j in [(c, j) for c in range(n_chunks) for j in range(heads_per_step)]:
        rows = pl.ds(c * CHUNK, CHUNK)
        lanes = slice(c * CHUNK, (c + 1) * CHUNK)
        hl = slice(j * LANES, (j + 1) * LANES)
        g_all, beta_all = gates[j]
        q = q_ref[0, rows, hl].astype(F32)
        k = k_ref[0, rows, hl].astype(F32)
        v = v_ref[0, rows, hl].astype(F32)
        g_row = g_all[:, lanes]
        g_col = jnp.sum(jnp.where(ci <= ri, g_row, 0.0), axis=1, keepdims=True)
        g_rowc = jnp.sum(jnp.where(ci == ri, g_col, 0.0), axis=0, keepdims=True)
        beta_col = jnp.sum(jnp.where(ci == ri, beta_all[:, lanes], 0.0), axis=1, keepdims=True)
        g_last = g_col[CHUNK - 1:CHUNK, :]
        decay = jnp.where(ci <= ri, jnp.exp(jnp.minimum(g_col - g_rowc, 0.0)), 0.0)
        e_g = jnp.exp(g_col)
        kb = k * beta_col
        chunks.append(dict(
            rows=rows, hl=hl, head=j, neg_l=jnp.where(ci < ri, -(_mm_nt(kb, k) * decay), 0.0),
            kbe=(kb * e_g).astype(BF16), vb=(v * beta_col).astype(BF16),
            attn=(_mm_nt(q, k) * decay).astype(BF16), qe=(q * e_g).astype(BF16),
            kd=(k * jnp.exp(g_last - g_col)).astype(BF16), gl=jnp.exp(g_last)))

    eye = jnp.where(ri == ci, 1.0, 0.0)
    ms = [ch["neg_l"] for ch in chunks]
    ps = [eye + m for m in ms]
    power = 1
    while 2 * power < CHUNK:
        ms = [_mm(m, m) for m in ms]
        ps = [p + _mm(p, m) for p, m in zip(ps, ms)]
        power *= 2

    ws = [_mm(p, ch["kbe"]).astype(BF16) for p, ch in zip(ps, chunks)]
    us = [_mm(p, ch["vb"]) for p, ch in zip(ps, chunks)]

    a_s = [(-_mm_tn(w, ch["kd"])).astype(BF16) for w, ch in zip(ws, chunks)]
    n_s = [_mm_tn(u, ch["kd"]) for u, ch in zip(us, chunks)]

    heads = range(heads_per_step)
    states = [s_ref[j] for j in heads]
    for c in range(n_chunks):
        idxs = [c * heads_per_step + j for j in heads]
        sbs = [states[j].astype(BF16) for j in heads]
        for idx, sb in zip(idxs, sbs):
            s0_ref[idx] = sb
        prods = [_mm(sb, a_s[idx]) for idx, sb in zip(idxs, sbs)]
        states = [states[j] * chunks[idx]["gl"] + prods[j] + n_s[idx] for j, idx in zip(heads, idxs)]
    for j in heads:
        s_ref[j] = states[j]

    wq_s = [_mm_nt(jnp.concatenate([w, ch["qe"]], axis=0), s0_ref[idx])
            for idx, (w, ch) in enumerate(zip(ws, chunks))]
    o_s = [wq[CHUNK:] + _mm(ch["attn"], u - wq[:CHUNK]) for wq, u, ch in zip(wq_s, us, chunks)]
    for ch, o in zip(chunks, o_s):
        z = z_ref[0, ch["rows"], ch["hl"]].astype(F32)
        o_ref[0, ch["rows"], ch["hl"]] = _head_rms_gate(o, nw, z).astype(o_ref.dtype)


def _deltanet_core(proj, abt, a_log, dt_bias, norm_w, *, time_tile=512, heads_per_step=8):
    bsz, t, _ = proj.shape
    n_heads = a_log.shape[0]
    dv = norm_w.shape[0]
    time_tile = min(time_tile, t)
    assert dv == LANES and proj.shape[2] == 4 * n_heads * LANES and t % time_tile == 0
    assert n_heads % heads_per_step == 0
    hh = n_heads // heads_per_step
    width = heads_per_step * LANES
    blk = lambda off: pl.BlockSpec((1, time_tile, width), lambda b, h, i: (b, i, off + h))
    per_head = pl.BlockSpec((heads_per_step, 1, LANES), lambda b, h, i: (h, 0, 0))
    bcast = lambda a: jnp.broadcast_to(a.astype(F32)[:, None, None], (n_heads, 1, LANES))
    return pl.pallas_call(
        functools.partial(_dn_kernel, n_chunks=time_tile // CHUNK, n_heads=n_heads, heads_per_step=heads_per_step),
        grid=(bsz, hh, t // time_tile),
        in_specs=[blk(0), blk(hh), blk(2 * hh), blk(3 * hh),
                  pl.BlockSpec((1, 2 * n_heads, time_tile), lambda b, h, i: (b, 0, i)),
                  per_head, per_head,
                  pl.BlockSpec((1, LANES), lambda b, h, i: (0, 0))],
        out_specs=pl.BlockSpec((1, time_tile, width), lambda b, h, i: (b, i, h)),
        out_shape=jax.ShapeDtypeStruct((bsz, t, n_heads * dv), BF16),
        scratch_shapes=[pltpu.VMEM((heads_per_step, LANES, LANES), F32),
                        pltpu.VMEM((time_tile // CHUNK * heads_per_step, LANES, LANES), BF16)],
        compiler_params=_params("parallel", "parallel", "arbitrary"),
        name="deltanet_chunk",
    )(proj, proj, proj, proj, abt, bcast(a_log), bcast(dt_bias), norm_w.reshape(1, dv))


def _deltanet_mixer(h, g, scale, shift, w_in, conv_w, a_log, dt_bias, norm_w):
    n_heads = a_log.shape[0]
    n_main = w_in.shape[1] - 2 * n_heads
    proj, abt = _dn_proj(h, g, scale, shift, w_in[:, :n_main].astype(BF16), w_in[:, n_main:].T, conv_w,
                         norm_w.shape[0])
    return _deltanet_core(proj, abt, a_log, dt_bias, norm_w)


def _gla_levels():
    levels, s = [], CHUNK // 2
    while s >= 1:
        levels.append(s)
        s //= 2
    return levels


def _gla_sum_matrix():
    c = CHUNK
    mats = [np.tril(np.ones((c, c), np.float32))]
    for s in _gla_levels():
        m_l = np.zeros((c, c), np.float32)
        for i in range(c):
            mid = (i // (2 * s)) * 2 * s + s
            if i >= mid:
                m_l[i, mid:i + 1] = 1.0
            else:
                m_l[i, i + 1:mid] = 1.0
        mats.append(m_l)
    return np.concatenate(mats, axis=0)


def _gla_kernel(q_ref, f_ref, i_ref, g_ref, lbl_ref, nw_ref, sm_ref, o_ref, st_ref, *, n_chunks, heads_per_step,
                layer):
    @pl.when(pl.program_id(2) == 0)
    def _():
        st_ref[...] = jnp.zeros_like(st_ref)

    logits = lbl_ref[...]
    e = jnp.exp(logits - jnp.max(logits, axis=0, keepdims=True))
    if layer == 0:
        lb = jnp.zeros((1, logits.shape[1]), F32)
    else:
        lb = jnp.sum(e[1:layer + 1], axis=0, keepdims=True) / jnp.sum(e, axis=0, keepdims=True)
    nw = nw_ref[...]
    sm = sm_ref[...]
    ri = lax.broadcasted_iota(jnp.int32, (CHUNK, CHUNK), 0)
    ci = lax.broadcasted_iota(jnp.int32, (CHUNK, CHUNK), 1)
    row = lax.broadcasted_iota(jnp.int32, (CHUNK, 1), 0)
    levels = _gla_levels()
    heads = range(heads_per_step)
    chunk_rows = lambda c: slice(c * CHUNK, (c + 1) * CHUNK)
    head_lanes = lambda j: slice(j * LANES, (j + 1) * LANES)

    q_all = _silu(q_ref[0].astype(F32))
    f_all = lb + (1.0 - lb) * _sigmoid(f_ref[0].astype(F32))
    k_all = 1.0 - f_all
    lf_hi, lf_lo = _split2(jnp.log(f_all))

    sums = [jnp.dot(sm, lf_hi[chunk_rows(c)], preferred_element_type=F32)
            + jnp.dot(sm, lf_lo[chunk_rows(c)], preferred_element_type=F32) for c in range(n_chunks)]

    scaled = []
    for c in range(n_chunks):
        q, k, sc = q_all[chunk_rows(c)], k_all[chunk_rows(c)], sums[c]
        b = sc[:CHUNK]
        b_last = b[CHUNK - 1:CHUNK, :]
        lv = []
        for l, s in enumerate(levels):
            ex = jnp.exp(sc[(l + 1) * CHUNK:(l + 2) * CHUNK])
            upper = (row & (2 * s - 1)) >= s
            lv.append((jnp.where(upper, q * ex, 0.0).astype(BF16), jnp.where(upper, 0.0, k * ex).astype(BF16)))
        scaled.append(dict(q=q.astype(BF16), k=k.astype(BF16), lv=lv, qb=(q * jnp.exp(b)).astype(BF16),
                           kd=(k * jnp.exp(b_last - b)).astype(BF16), f_last=jnp.exp(b_last),
                           v=i_ref[0, chunk_rows(c), :].astype(BF16)))
    items = [(c, j) for c in range(n_chunks) for j in heads]

    a_s = [jnp.where(ri == ci, _mm_nt(scaled[c]["q"][:, head_lanes(j)], scaled[c]["k"][:, head_lanes(j)]), 0.0)
           for c, j in items]
    for l, s in enumerate(levels):
        same = (ri // (2 * s)) == (ci // (2 * s))
        parts = [_mm_nt(scaled[c]["lv"][l][0][:, head_lanes(j)], scaled[c]["lv"][l][1][:, head_lanes(j)])
                 for c, j in items]
        a_s = [a + jnp.where(same, part, 0.0) for a, part in zip(a_s, parts)]

    kv_s = [_mm_tn(scaled[c]["v"][:, head_lanes(j)], scaled[c]["kd"][:, head_lanes(j)]) for c, j in items]
    states = [st_ref[j] for j in heads]
    starts = []
    for idx, (c, j) in enumerate(items):
        starts.append(states[j].astype(BF16))
        states[j] = states[j] * scaled[c]["f_last"][:, head_lanes(j)] + kv_s[idx]
    for j in heads:
        st_ref[j] = states[j]

    inter = [_mm_nt(scaled[c]["qb"][:, head_lanes(j)], st0) for (c, j), st0 in zip(items, starts)]
    o_s = [x + _mm(a, scaled[c]["v"][:, head_lanes(j)]) for x, a, (c, j) in zip(inter, a_s, items)]
    for (c, j), o in zip(items, o_s):
        g = g_ref[0, chunk_rows(c), head_lanes(j)].astype(F32)
        o_ref[0, chunk_rows(c), head_lanes(j)] = _head_rms_gate(o, nw, g).astype(o_ref.dtype)


def _gla_core(proj, lb_logits, layer, norm_w, *, time_tile=512, heads_per_step=4):
    bsz, t, n = proj.shape
    dv = norm_w.shape[0]
    n_heads = n // (4 * LANES)
    time_tile = min(time_tile, t)
    assert dv == LANES and n == 4 * n_heads * LANES and t % time_tile == 0 and n_heads % heads_per_step == 0
    hh = n_heads // heads_per_step
    width = heads_per_step * LANES
    blk = lambda off: pl.BlockSpec((1, time_tile, width), lambda b, h, i: (b, i, off + h))
    sm = jnp.asarray(_gla_sum_matrix(), BF16)
    return pl.pallas_call(
        functools.partial(_gla_kernel, n_chunks=time_tile // CHUNK, heads_per_step=heads_per_step, layer=layer),
        grid=(bsz, hh, t // time_tile),
        in_specs=[blk(0), blk(hh), blk(2 * hh), blk(3 * hh),
                  pl.BlockSpec((lb_logits.shape[0], width), lambda b, h, i: (0, h)),
                  pl.BlockSpec((1, LANES), lambda b, h, i: (0, 0)),
                  pl.BlockSpec(sm.shape, lambda b, h, i: (0, 0))],
        out_specs=pl.BlockSpec((1, time_tile, width), lambda b, h, i: (b, i, h)),
        out_shape=jax.ShapeDtypeStruct((bsz, t, n_heads * dv), BF16),
        scratch_shapes=[pltpu.VMEM((heads_per_step, LANES, LANES), F32)],
        compiler_params=_params("parallel", "parallel", "arbitrary"),
        name="gla_chunk",
    )(proj, proj, proj, proj, lb_logits.astype(F32), norm_w.reshape(1, dv), sm)


def _hgrn2_mixer(h, g, scale, shift, w_in, lb_logits, layer, norm_w):
    proj = _proj(h, g, scale, shift, w_in.astype(BF16))
    return _gla_core(proj, lb_logits, layer, norm_w)


def _rwkv_proj_kernel(h_ref, g_ref, sc_ref, sh_ref, mu_ref, wrkv_ref, w0_ref, w1_ref, w2_ref, a0_ref, a1_ref,
                      a2_ref, g1_ref, g2_ref, r_ref, k_ref, v_ref, a_ref, wl_ref, gate_ref, carry_ref):
    @pl.when(pl.program_id(1) == 0)
    def _():
        carry_ref[...] = jnp.zeros_like(carry_ref)

    u = _norm_mod(h_ref[0], g_ref[...], sc_ref[0], sh_ref[0])
    dx = _shift_rows(u, carry_ref[...], 1) - u
    carry_ref[...] = u[-SUBLANES:]
    mu = mu_ref[...]
    mix = lambda s: (u + dx * mu[s:s + 1]).astype(BF16)
    dot = lambda a, b: jnp.dot(a, b, preferred_element_type=F32)
    r_ref[0] = dot(mix(0), wrkv_ref[0]).astype(r_ref.dtype)
    k_ref[0] = dot(mix(2), wrkv_ref[1]).astype(k_ref.dtype)
    v_ref[0] = dot(mix(3), wrkv_ref[2]).astype(v_ref.dtype)
    w = w0_ref[...] + dot(jnp.tanh(dot(mix(1), w1_ref[...])).astype(BF16), w2_ref[...])
    w = -_softplus(-w) - 0.5
    wl_ref[0] = -jnp.exp(w)
    a_ref[0] = _sigmoid(a0_ref[...] + dot(dot(mix(4), a1_ref[...]).astype(BF16), a2_ref[...]))
    gate_ref[0] = dot(_sigmoid(dot(mix(5), g1_ref[...])).astype(BF16), g2_ref[...]).astype(gate_ref.dtype)


def _pad_lora(w_in, w_out):
    rank = w_in.shape[1]
    pad = -(-rank // LANES) * LANES - rank
    return (jnp.pad(w_in, ((0, 0), (0, pad))).astype(BF16), jnp.pad(w_out, ((0, pad), (0, 0))).astype(BF16))


def _rwkv_proj(h, g, scale, shift, mu, w_rkv, w0, w1, w2, a0, a1, a2, g1, g2, *, row_tile=512):
    bsz, t, d = h.shape
    row_tile = min(row_tile, t)
    assert t % row_tile == 0
    w1p, w2p = _pad_lora(w1, w2)
    a1p, a2p = _pad_lora(a1, a2)
    g1p, g2p = _pad_lora(g1, g2)
    const2 = lambda b, i: (0, 0)
    per_b = lambda b, i: (b, 0, 0)
    tile = pl.BlockSpec((1, row_tile, d), lambda b, i: (b, i, 0))
    full = lambda a: pl.BlockSpec(a.shape, const2)
    vec = pl.BlockSpec((1, d), const2)
    f32_out = jax.ShapeDtypeStruct((bsz, t, d), F32)
    bf16_out = jax.ShapeDtypeStruct((bsz, t, d), BF16)
    return pl.pallas_call(
        _rwkv_proj_kernel,
        grid=(bsz, t // row_tile),
        in_specs=[tile, vec, pl.BlockSpec((1, 1, d), per_b), pl.BlockSpec((1, 1, d), per_b),
                  full(mu), pl.BlockSpec(w_rkv.shape, lambda b, i: (0, 0, 0)),
                  vec, full(w1p), full(w2p), vec, full(a1p), full(a2p), full(g1p), full(g2p)],
        out_specs=[tile] * 6,
        out_shape=[bf16_out] * 3 + [f32_out] * 2 + [bf16_out],
        scratch_shapes=[pltpu.VMEM((SUBLANES, d), F32)],
        compiler_params=_params("parallel", "arbitrary"),
        name="rwkv_proj",
    )(h, g.reshape(1, d), scale.reshape(bsz, 1, d), shift.reshape(bsz, 1, d), mu, w_rkv.astype(BF16),
      w0.reshape(1, d), w1p, w2p, a0.reshape(1, d), a1p, a2p, g1p, g2p)


def _rwkv_chunk_kernel(r_ref, k_ref, v_ref, a_ref, wl_ref, gate_ref, kk_ref, ka_ref, rk_ref, lnw_ref, lnb_ref,
                       tri_ref, o_ref, s_ref, at_ref, rt_ref, bt_ref, kt_ref, bonus_ref, y_ref, s0_ref,
                       *, n_chunks, pairs_per_step, head_size, gn_eps):
    @pl.when(pl.program_id(2) == 0)
    def _():
        s_ref[...] = jnp.zeros_like(s_ref)

    c_len = CHUNK
    lane = lax.broadcasted_iota(jnp.int32, (1, LANES), 1)
    lo = lane < head_size
    lane_t = lax.broadcasted_iota(jnp.int32, (c_len, LANES), 1) & (head_size - 1)
    row_t = lax.broadcasted_iota(jnp.int32, (c_len, LANES), 0)
    strict = lane_t < row_t
    incl = lane_t <= row_t
    eye2 = jnp.where(lane_t == row_t, 1.0, 0.0)
    bd_r = lax.broadcasted_iota(jnp.int32, (LANES, LANES), 0) < head_size
    bd_c = lax.broadcasted_iota(jnp.int32, (LANES, LANES), 1) < head_size
    block_diag = bd_r == bd_c
    tri = tri_ref[...]

    def pair_sum(x):
        s_lo = jnp.sum(jnp.where(lo, x, 0.0), axis=-1, keepdims=True)
        s_hi = jnp.sum(jnp.where(lo, 0.0, x), axis=-1, keepdims=True)
        return jnp.where(lo, s_lo, s_hi)

    def stack(x, first_lo=True):
        x_lo, x_hi = jnp.where(lo, x, 0.0), jnp.where(lo, 0.0, x)
        return jnp.concatenate([x_lo, x_hi] if first_lo else [x_hi, x_lo], axis=0)

    pairs = range(pairs_per_step)
    lanes_of = lambda p: slice(p * LANES, (p + 1) * LANES)
    kk_all, k_all = [], []
    for p in pairs:
        pl_ = lanes_of(p)
        k_raw = k_ref[0, :, pl_]
        a = a_ref[0, :, pl_]
        kx = k_raw * kk_ref[:, pl_]
        kk_all.append(kx * lax.rsqrt(pair_sum(kx * kx) + 1e-6))
        k_all.append(k_raw * (1.0 + (a - 1.0) * ka_ref[:, pl_]))
        bonus_ref[:, pl_] = pair_sum(r_ref[0, :, pl_] * k_all[p] * rk_ref[:, pl_]) * v_ref[0, :, pl_]

    cums = []
    for c in range(n_chunks):
        wl_hi, wl_lo = _split2(wl_ref[0, pl.ds(c * c_len, c_len), :])
        cums.append(jnp.dot(tri, wl_hi, preferred_element_type=F32)
                    + jnp.dot(tri, wl_lo, preferred_element_type=F32))
    cum_all = jnp.concatenate(cums, axis=0)
    gam_last = [jnp.exp(cm[c_len - 1:c_len, :]) for cm in cums]

    for p in pairs:
        pl_ = lanes_of(p)
        cum = cum_all[:, pl_]
        inv_gam = jnp.exp(-cum)
        at_ref[:, pl_] = (-kk_all[p] * jnp.exp(cum - wl_ref[0, :, pl_])).astype(BF16)
        rt_ref[:, pl_] = (r_ref[0, :, pl_] * jnp.exp(cum)).astype(BF16)
        bt_ref[:, pl_] = (kk_all[p] * a_ref[0, :, pl_] * inv_gam).astype(BF16)
        kt_ref[:, pl_] = (k_all[p] * inv_gam).astype(BF16)

    items = []
    for c, p in [(c, p) for c in range(n_chunks) for p in pairs]:
        rows = pl.ds(c * c_len, c_len)
        pl_ = lanes_of(p)
        a_t = at_ref[rows, pl_]
        r_t = rt_ref[rows, pl_]
        b_t = bt_ref[rows, pl_]
        k_t = kt_ref[rows, pl_]
        v = v_ref[0, rows, pl_]
        bk = jnp.concatenate([b_t, k_t], axis=0)
        ar = jnp.concatenate([a_t, r_t], axis=0)
        zero = jnp.zeros_like(ar)
        r1 = _mm_nt(jnp.where(lo, ar, zero), bk)
        r2 = _mm_nt(jnp.where(lo, zero, ar), jnp.concatenate([k_t, b_t], axis=0))
        a_ab = jnp.where(strict, jnp.where(lo, r1[:c_len], r2[:c_len]), 0.0)
        a_ak = jnp.where(strict, jnp.where(lo, r2[:c_len], r1[:c_len]), 0.0)
        a_rb = jnp.where(incl, jnp.where(lo, r1[c_len:], r2[c_len:]), 0.0)
        a_rk = jnp.where(incl, jnp.where(lo, r2[c_len:], r1[c_len:]), 0.0)
        items.append(dict(
            rows=rows, pl=pl_, pair=p, a_ab=a_ab, a_ak=a_ak.astype(BF16),
            a_r=jnp.concatenate([a_rb, a_rk], axis=1).astype(BF16),
            a_t=a_t, r_t=r_t, bk=bk, v_sw=stack(v, first_lo=False).astype(BF16), v=v,
            gl_row=gam_last[c][:, pl_]))

    ms = [it["a_ab"] for it in items]
    ts = [eye2 + m for m in ms]
    power = 1
    while 2 * power < c_len:
        ms = [_mm(m, stack(m)) for m in ms]
        ts = [t + _mm(t, stack(m)) for t, m in zip(ts, ms)]
        power *= 2

    w_ts = [_mm(t, stack(it["a_t"])).astype(BF16) for t, it in zip(ts, items)]
    akvs = [_mm(it["a_ak"], it["v_sw"]) for it in items]
    u_0s = [_mm(t, stack(akv)) for t, akv in zip(ts, akvs)]

    m_s = [(jnp.where(block_diag, _mm_tn(w_t, it["bk"][:c_len]), 0.0) * it["gl_row"]).astype(BF16)
           for w_t, it in zip(w_ts, items)]
    n_s = [jnp.where(block_diag, _mm_tn(jnp.concatenate([u_0, it["v"]], axis=0), it["bk"]), 0.0) * it["gl_row"]
           for u_0, it in zip(u_0s, items)]

    states = [s_ref[p] for p in pairs]
    for c in range(n_chunks):
        idxs = [c * pairs_per_step + p for p in pairs]
        sbs = [states[p].astype(BF16) for p in pairs]
        for idx, sb in zip(idxs, sbs):
            s0_ref[idx] = sb
        prods = [_mm(sb, m_s[idx]) for idx, sb in zip(idxs, sbs)]
        states = [states[p] * items[idx]["gl_row"] + prods[p] + n_s[idx] for p, idx in zip(pairs, idxs)]
    for p in pairs:
        s_ref[p] = states[p]

    wr_s = [_mm_nt(jnp.concatenate([w_t, it["r_t"]], axis=0), s0_ref[idx])
            for idx, (it, w_t) in enumerate(zip(items, w_ts))]
    y_s = [wr[c_len:] + _mm(it["a_r"], jnp.concatenate([stack(wr[:c_len] + u_0).astype(BF16), it["v_sw"]], axis=0))
           for it, wr, u_0 in zip(items, wr_s, u_0s)]
    for it, y in zip(items, y_s):
        y_ref[it["rows"], it["pl"]] = y

    inv_n = 1.0 / head_size
    for p in pairs:
        pl_ = lanes_of(p)
        y = y_ref[:, pl_]
        mean = pair_sum(y) * inv_n
        yc = y - mean
        var = pair_sum(yc * yc) * inv_n
        y_n = yc * lax.rsqrt(var + gn_eps) * lnw_ref[:, pl_] + lnb_ref[:, pl_]
        o_ref[0, :, pl_] = ((y_n + bonus_ref[:, pl_]) * gate_ref[0, :, pl_].astype(F32)).astype(o_ref.dtype)


def _rwkv_chunk(r, k, v, a, wl, gate, k_k, k_a, r_k, ln_w, ln_b, *, time_tile=512, pairs_per_step=8):
    bsz, t, d = r.shape
    head_size = r_k.shape[1]
    assert 2 * head_size == LANES and d % (LANES * pairs_per_step) == 0
    time_tile = min(time_tile, t)
    assert t % time_tile == 0
    width = LANES * pairs_per_step
    n_pairs = d // width
    blk = pl.BlockSpec((1, time_tile, width), lambda b, p, i: (b, i, p))
    vec = pl.BlockSpec((1, width), lambda b, p, i: (0, p))
    tri = jnp.asarray(np.tril(np.ones((CHUNK, CHUNK), np.float32)), BF16)
    flat = lambda x: x.reshape(1, d).astype(F32)
    return pl.pallas_call(
        functools.partial(_rwkv_chunk_kernel, n_chunks=time_tile // CHUNK, pairs_per_step=pairs_per_step,
                          head_size=head_size, gn_eps=1e-5 * head_size * head_size),
        grid=(bsz, n_pairs, t // time_tile),
        in_specs=[blk] * 6 + [vec] * 5 + [pl.BlockSpec((CHUNK, CHUNK), lambda b, p, i: (0, 0))],
        out_specs=blk,
        out_shape=jax.ShapeDtypeStruct((bsz, t, d), BF16),
        scratch_shapes=[pltpu.VMEM((pairs_per_step, LANES, LANES), F32)]
        + [pltpu.VMEM((time_tile, width), BF16)] * 4 + [pltpu.VMEM((time_tile, width), F32)] * 2
        + [pltpu.VMEM((time_tile // CHUNK * pairs_per_step, LANES, LANES), BF16)],
        compiler_params=_params("parallel", "parallel", "arbitrary"),
        name="rwkv_chunk",
    )(r, k, v, a, wl, gate, flat(k_k), flat(k_a), flat(r_k), flat(ln_w), flat(ln_b), tri)


def _rwkv7_mixer(h, g, scale, shift, mu, w_rkv, w0, w1, w2, a0, a1, a2, g1, g2, k_k, k_a, r_k, ln_w, ln_b):
    r, k, v, a, wl, gt = _rwkv_proj(h, g, scale, shift, mu, w_rkv, w0, w1, w2, a0, a1, a2, g1, g2)
    return _rwkv_chunk(r, k, v, a, wl, gt, k_k, k_a, r_k, ln_w, ln_b)


def kernel(x, c, norm_g, ada_w, ada_b, a_w_in, a_conv, a_A_log, a_dt_bias, a_norm, a_w_out, b_w_in, hgrn_lb_logits, b_norm, b_w_out, c_mu, c_w_rkv, c_w0, c_w1, c_w2, c_a0, c_a1, c_a2, c_g1, c_g2, c_k_k, c_k_a, c_r_k, c_ln_w, c_ln_b, c_w_out, f_w_up, f_conv_w, f_conv_b, f_w_down, final_g):
    depth = ada_w.shape[0]
    n_mixers = 3
    mod = _ada_mod(c, ada_w, ada_b)
    w_up_all = f_w_up.astype(BF16)
    w_down_all = f_w_down.astype(BF16)
    h = x
    for i in range(depth):
        shift1, scale1, gate1, shift2, scale2, gate2 = (mod[i, :, s] for s in range(6))
        kind, j = i % n_mixers, i // n_mixers
        if kind == 0:
            y = _deltanet_mixer(h, norm_g[i, 0], scale1, shift1, a_w_in[j], a_conv[j], a_A_log[j], a_dt_bias[j],
                                a_norm[j])
            w_out = a_w_out[j]
        elif kind == 1:
            y = _hgrn2_mixer(h, norm_g[i, 0], scale1, shift1, b_w_in[j], hgrn_lb_logits, i, b_norm[j])
            w_out = b_w_out[j]
        else:
            y = _rwkv7_mixer(h, norm_g[i, 0], scale1, shift1, c_mu[j], c_w_rkv[j], c_w0[j], c_w1[j], c_w2[j],
                             c_a0[j], c_a1[j], c_a2[j], c_g1[j], c_g2[j], c_k_k[j], c_k_a[j], c_r_k[j], c_ln_w[j],
                             c_ln_b[j])
            w_out = c_w_out[j]
        h = _mix_out_ffn(y, w_out, gate1, h, norm_g[i, 1], scale2, shift2, gate2, w_up_all, f_conv_w[i],
                         f_conv_b[i], w_down_all, i, final_g if i == depth - 1 else None)
    return h
```
